```python
import jax
import jax.numpy as jnp
from jax import lax
import numpy as np

D_MODEL = 2048
BATCH = 4
SEQ = 4096
DEPTH = 2

CTX_LEN = 256
GRID_W = 64
N_EVEN = (DEPTH + 1) // 2
N_ODD = DEPTH // 2
N_MOD = 6
RMS_EPS = 1e-6

V_HEAD = 128
QK_NOPE = 128
QK_ROPE = 64
QK_HEAD = QK_NOPE + QK_ROPE
MLA_HEADS = D_MODEL // (2 * V_HEAD)
Q_LORA = 512
KV_LORA = 512
ROPE_AXIS = QK_ROPE // 2
ROPE_BASE = 10000.0
Q_BLOCK = 128

POOL_WINDOWS = (2, 4, 8, 16)
POOL_W = D_MODEL // 2
POOL_GC = POOL_W // len(POOL_WINDOWS)

EVEN_IN = Q_LORA + KV_LORA + QK_ROPE + POOL_W
EVEN_MIX = MLA_HEADS * V_HEAD + POOL_W

LRU_W = D_MODEL
LRU_BLOCKS = 8
LRU_BS = LRU_W // LRU_BLOCKS
CONV_W = 4
LRU_C = 8.0

N_EXPERTS = 16
EC_FACTOR = 2
EXPERT_FF = 2048

kernel_name = "hybrid_mla_pool_rglru_ecmoe_diffusion"


def _rmsnorm(x, g):
    xf = x.astype(jnp.float32)
    y = xf * lax.rsqrt(jnp.mean(xf * xf, axis=-1, keepdims=True) + RMS_EPS)
    return (y * g.astype(jnp.float32)).astype(x.dtype)


def _rope_tables(rows):
    row = jnp.repeat(jnp.arange(rows, dtype=jnp.float32), GRID_W)
    col = jnp.tile(jnp.arange(GRID_W, dtype=jnp.float32), rows)
    inv = ROPE_BASE ** (-jnp.arange(0, ROPE_AXIS, 2, dtype=jnp.float32) / ROPE_AXIS)
    ar = row[:, None] * inv
    ac = col[:, None] * inv
    cos = jnp.concatenate([jnp.cos(ar), jnp.cos(ar), jnp.cos(ac), jnp.cos(ac)], axis=-1)
    sin = jnp.concatenate([jnp.sin(ar), jnp.sin(ar), jnp.sin(ac), jnp.sin(ac)], axis=-1)
    return cos, sin


def _apply_rope(x, cos, sin):
    half = ROPE_AXIS // 2
    rot = jnp.concatenate([-x[..., half:ROPE_AXIS], x[..., :half],
                           -x[..., ROPE_AXIS + half:], x[..., ROPE_AXIS:ROPE_AXIS + half]], axis=-1)
    return x * cos.astype(x.dtype) + rot * sin.astype(x.dtype)


def _attend(q, k, v):
    b, t, h, dk = q.shape
    nb = t // Q_BLOCK
    qb = jnp.moveaxis(q.reshape(b, nb, Q_BLOCK, h, dk), 1, 0)
    scale = QK_HEAD ** -0.5

    def one_block(qi):
        s = jnp.einsum("bqhd,bkhd->bhqk", qi, k).astype(jnp.float32) * scale
        p = jax.nn.softmax(s, axis=-1).astype(v.dtype)
        return jnp.einsum("bhqk,bkhd->bqhd", p, v)

    o = lax.map(one_block, qb)
    return jnp.moveaxis(o, 0, 1).reshape(b, t, h, v.shape[-1])


def _mla_q(proj, g_qa, w_uq):
    b, t, _ = proj.shape
    return (_rmsnorm(proj[..., :Q_LORA], g_qa) @ w_uq).reshape(b, t, MLA_HEADS, QK_HEAD)


def _mla_kv(proj, g_kv, w_ukv, rope):
    b, t, _ = proj.shape
    kv_a = proj[..., Q_LORA:Q_LORA + KV_LORA + QK_ROPE]
    c_kv = _rmsnorm(kv_a[..., :KV_LORA], g_kv)
    k_pe = kv_a[..., KV_LORA:]
    if rope is not None:
        k_pe = _apply_rope(k_pe, rope[0], rope[1])
    kv = (c_kv @ w_ukv).reshape(b, t, MLA_HEADS, QK_NOPE + V_HEAD)
    k = jnp.concatenate([kv[..., :QK_NOPE],
                         jnp.broadcast_to(k_pe[:, :, None, :], (b, t, MLA_HEADS, QK_ROPE))], axis=-1)
    return k, kv[..., QK_NOPE:]


def _multiscale_pool(u, w_pool, pool_scale):
    b, t, _ = u.shape
    uf = u.astype(jnp.float32)
    cs = jnp.concatenate([jnp.zeros((b, 1, POOL_W), jnp.float32), jnp.cumsum(uf, axis=1)], axis=1)
    pos = jnp.arange(t)
    means = []
    for g, w in enumerate(POOL_WINDOWS):
        lo = jnp.clip(pos - w // 2, 0, t)
        hi = jnp.clip(pos + w - w // 2, 0, t)
        seg = cs[:, :, g * POOL_GC:(g + 1) * POOL_GC]
        means.append((seg[:, hi] - seg[:, lo]) / (hi - lo).astype(jnp.float32)[:, None])
    pooled = (jnp.concatenate(means, axis=-1) - uf).astype(u.dtype)
    y = jnp.einsum("btgc,gcd->btgd", pooled.reshape(b, t, len(POOL_WINDOWS), POOL_GC), w_pool)
    return y.reshape(b, t, POOL_W) * pool_scale


def _even_mixer(h_lat, h_ctx, w_in, g_qa, w_uq, g_kv, w_ukv, w_pool, pool_scale, w_out, rope, need_ctx):
    b, t, _ = h_lat.shape
    cos, sin = rope
    p_lat = h_lat @ w_in
    p_ctx = h_ctx @ w_in
    q_lat = _mla_q(p_lat, g_qa, w_uq)
    q_lat = jnp.concatenate([q_lat[..., :QK_NOPE],
                             _apply_rope(q_lat[..., QK_NOPE:], cos[:, None, :], sin[:, None, :])], axis=-1)
    k_lat, v_lat = _mla_kv(p_lat, g_kv, w_ukv, rope)
    k_ctx, v_ctx = _mla_kv(p_ctx, g_kv, w_ukv, None)
    a_lat = _attend(q_lat, jnp.concatenate([k_ctx, k_lat], axis=1), jnp.concatenate([v_ctx, v_lat], axis=1))
    pool_lat = _multiscale_pool(p_lat[..., EVEN_IN - POOL_W:], w_pool, pool_scale)
    out_lat = jnp.concatenate([a_lat.reshape(b, t, MLA_HEADS * V_HEAD), pool_lat], axis=-1) @ w_out
    if not need_ctx:
        return out_lat, None
    a_ctx = _attend(_mla_q(p_ctx, g_qa, w_uq), k_ctx, v_ctx)
    pool_ctx = _multiscale_pool(p_ctx[..., EVEN_IN - POOL_W:], w_pool, pool_scale)
    out_ctx = jnp.concatenate([a_ctx.reshape(b, h_ctx.shape[1], MLA_HEADS * V_HEAD), pool_ctx], axis=-1) @ w_out
    return out_lat, out_ctx


def _dwconv(u, w, bias):
    y = lax.conv_general_dilated(u, w[:, None, :], window_strides=(1,),
                                 padding=[(CONV_W // 2, CONV_W - 1 - CONV_W // 2)],
                                 dimension_numbers=("NWC", "WIO", "NWC"),
                                 feature_group_count=u.shape[-1])
    return y + bias


def _blockdiag(u, w):
    b, t, _ = u.shape
    return jnp.einsum("btnc,ncd->btnd", u.reshape(b, t, LRU_BLOCKS, LRU_BS), w).reshape(b, t, LRU_W)


def _lin_combine(e1, e2):
    a1, b1 = e1
    a2, b2 = e2
    return a1 * a2, a2 * b1 + b2


def _rglru_scan(u, h0, w_a, b_a, w_x, b_x, lam):
    uf = u.astype(jnp.float32)
    r = jax.nn.sigmoid((_blockdiag(u, w_a) + b_a).astype(jnp.float32))
    i = jax.nn.sigmoid((_blockdiag(u, w_x) + b_x).astype(jnp.float32))
    log_a = -LRU_C * r * jax.nn.softplus(-lam.astype(jnp.float32))
    a = jnp.exp(log_a)
    bx = jnp.sqrt(-jnp.expm1(2.0 * log_a)) * (i * uf)
    bx = bx.at[:, 0].add(a[:, 0] * h0)
    _, h = lax.associative_scan(_lin_combine, (a, bx), axis=1)
    return h


def _odd_mixer(h_lat, h_ctx, w_in, conv_w, conv_b, w_a, b_a, w_x, b_x, lam, w_out, need_ctx):
    b = h_lat.shape[0]
    p_lat = h_lat @ w_in
    gate_lat = jax.nn.gelu(p_lat[..., :LRU_W])
    u_lat = _dwconv(p_lat[..., LRU_W:], conv_w, conv_b)
    p_ctx = h_ctx @ w_in if need_ctx else h_ctx @ w_in[:, LRU_W:]
    u_ctx = _dwconv(p_ctx[..., p_ctx.shape[-1] - LRU_W:], conv_w, conv_b)
    h0 = jnp.zeros((b, LRU_W), jnp.float32)
    y_lat = jnp.zeros(u_lat.shape, jnp.float32)
    y_ctx = jnp.zeros(u_ctx.shape, jnp.float32)
    for d in range(2):
        rev = (lambda s: jnp.flip(s, axis=1)) if d == 1 else (lambda s: s)
        hc = _rglru_scan(rev(u_ctx), h0, w_a[d], b_a[d], w_x[d], b_x[d], lam[d])
        hl = _rglru_scan(rev(u_lat), hc[:, -1], w_a[d], b_a[d], w_x[d], b_x[d], lam[d])
        y_lat = y_lat + rev(hl)
        if need_ctx:
            y_ctx = y_ctx + rev(hc)
    out_lat = (y_lat.astype(h_lat.dtype) * gate_lat) @ w_out
    if not need_ctx:
        return out_lat, None
    out_ctx = (y_ctx.astype(h_ctx.dtype) * jax.nn.gelu(p_ctx[..., :LRU_W])) @ w_out
    return out_lat, out_ctx


def _expert_choice(h, w_router, w1, w3, w2):
    n_tok, d = h.shape[1], h.shape[2]
    cap = EC_FACTOR * n_tok // N_EXPERTS
    aff = jax.nn.softmax(jnp.einsum("btd,de->bte", h, w_router).astype(jnp.float32), axis=-1)
    g, idx = lax.top_k(jnp.swapaxes(aff, 1, 2), cap)
    xs = jax.vmap(lambda hb, ib: hb[ib])(h, idx)
    a1 = jnp.einsum("becd,edf->becf", xs, w1)
    a3 = jnp.einsum("becd,edf->becf", xs, w3)
    y = jnp.einsum("becf,efd->becd", jax.nn.silu(a1) * a3, w2) * g[..., None].astype(h.dtype)
    return jax.vmap(lambda ib, yb: jnp.zeros((n_tok, d), yb.dtype).at[ib.reshape(-1)].add(yb.reshape(-1, d)))(idx, y)


def setup_inputs(seed: int = 0) -> dict:
    key = jax.random.key(seed)
    ks = iter(jax.random.split(key, 32))

    def nrm(shape, scale):
        return jax.random.normal(next(ks), shape, jnp.float32) * scale

    a0 = jax.random.uniform(next(ks), (N_ODD, 2, LRU_W), jnp.float32, 0.9, 0.999)
    p0 = a0 ** (1.0 / LRU_C)
    return {
        "x": nrm((BATCH, SEQ, D_MODEL), 1.0),
        "c": nrm((BATCH, D_MODEL), 1.0),
        "ctx": nrm((BATCH, CTX_LEN, D_MODEL), 1.0),
        "c_ctx": nrm((D_MODEL,), 1.0),
        "w_ada": nrm((DEPTH, D_MODEL, N_MOD * D_MODEL), D_MODEL ** -0.5),
        "b_ada": nrm((DEPTH, N_MOD * D_MODEL), 0.01),
        "norm_g": 1.0 + nrm((DEPTH, 2, D_MODEL), 0.02),
        "final_g": 1.0 + nrm((D_MODEL,), 0.02),
        "w_in_e": nrm((N_EVEN, D_MODEL, EVEN_IN), D_MODEL ** -0.5),
        "g_qa": 1.0 + nrm((N_EVEN, Q_LORA), 0.02),
        "w_uq": nrm((N_EVEN, Q_LORA, MLA_HEADS * QK_HEAD), Q_LORA ** -0.5),
        "g_kv": 1.0 + nrm((N_EVEN, KV_LORA), 0.02),
        "w_ukv": nrm((N_EVEN, KV_LORA, MLA_HEADS * (QK_NOPE + V_HEAD)), KV_LORA ** -0.5),
        "w_pool": nrm((N_EVEN, len(POOL_WINDOWS), POOL_GC, POOL_GC), POOL_GC ** -0.5),
        "pool_scale": 1.0 + nrm((N_EVEN, POOL_W), 0.02),
        "w_out_e": nrm((N_EVEN, EVEN_MIX, D_MODEL), EVEN_MIX ** -0.5),
        "w_in_o": nrm((N_ODD, D_MODEL, 2 * LRU_W), D_MODEL ** -0.5),
        "conv_w": nrm((N_ODD, CONV_W, LRU_W), CONV_W ** -0.5),
        "conv_b": nrm((N_ODD, LRU_W), 0.01),
        "w_rg_a": nrm((N_ODD, 2, LRU_BLOCKS, LRU_BS, LRU_BS), LRU_BS ** -0.5),
        "b_rg_a": nrm((N_ODD, 2, LRU_W), 0.01),
        "w_rg_x": nrm((N_ODD, 2, LRU_BLOCKS, LRU_BS, LRU_BS), LRU_BS ** -0.5),
        "b_rg_x": nrm((N_ODD, 2, LRU_W), 0.01),
        "lru_lambda": jnp.log(p0) - jnp.log1p(-p0),
        "w_out_o": nrm((N_ODD, LRU_W, D_MODEL), LRU_W ** -0.5),
        "w_router": nrm((DEPTH, D_MODEL, N_EXPERTS), D_MODEL ** -0.5),
        "w_e1": nrm((DEPTH, N_EXPERTS, D_MODEL, EXPERT_FF), D_MODEL ** -0.5),
        "w_e3": nrm((DEPTH, N_EXPERTS, D_MODEL, EXPERT_FF), D_MODEL ** -0.5),
        "w_e2": nrm((DEPTH, N_EXPERTS, EXPERT_FF, D_MODEL), EXPERT_FF ** -0.5),
    }


def reference(x, c, ctx, c_ctx, w_ada, b_ada, norm_g, final_g,
              w_in_e, g_qa, w_uq, g_kv, w_ukv, w_pool, pool_scale, w_out_e,
              w_in_o, conv_w, conv_b, w_rg_a, b_rg_a, w_rg_x, b_rg_x, lru_lambda, w_out_o,
              w_router, w_e1, w_e3, w_e2):
    n_tok = x.shape[1]
    ROWS = n_tok // GRID_W
    rope = _rope_tables(ROWS)
    silu_c = jax.nn.silu(c)
    silu_cc = jax.nn.silu(c_ctx)
    cs = ctx
    for layer in range(DEPTH):
        last = layer == DEPTH - 1
        mod_lat = (silu_c @ w_ada[layer] + b_ada[layer])[:, None, :]
        mod_ctx = silu_cc @ w_ada[layer] + b_ada[layer]
        sh_m, sc_m, gt_m, sh_f, sc_f, gt_f = jnp.split(mod_lat, N_MOD, axis=-1)
        csh_m, csc_m, cgt_m, csh_f, csc_f, cgt_f = jnp.split(mod_ctx, N_MOD, axis=-1)
        h_lat = _rmsnorm(x, norm_g[layer, 0]) * (1.0 + sc_m) + sh_m
        h_ctx = _rmsnorm(cs, norm_g[layer, 0]) * (1.0 + csc_m) + csh_m
        if layer % 2 == 0:
            e = layer // 2
            m_lat, m_ctx = _even_mixer(h_lat, h_ctx, w_in_e[e], g_qa[e], w_uq[e], g_kv[e], w_ukv[e],
                                       w_pool[e], pool_scale[e], w_out_e[e], rope, not last)
        else:
            o = layer // 2
            m_lat, m_ctx = _odd_mixer(h_lat, h_ctx, w_in_o[o], conv_w[o], conv_b[o], w_rg_a[o], b_rg_a[o],
                                      w_rg_x[o], b_rg_x[o], lru_lambda[o], w_out_o[o], not last)
        x = x + gt_m * m_lat
        x = x + gt_f * _expert_choice(_rmsnorm(x, norm_g[layer, 1]) * (1.0 + sc_f) + sh_f,
                                      w_router[layer], w_e1[layer], w_e3[layer], w_e2[layer])
        if not last:
            cs = cs + cgt_m * m_ctx
            cs = cs + cgt_f * _expert_choice(_rmsnorm(cs, norm_g[layer, 1]) * (1.0 + csc_f) + csh_f,
                                             w_router[layer], w_e1[layer], w_e3[layer], w_e2[layer])
    return _rmsnorm(x, final_g)
```

```python
import functools
import math

import jax
import jax.numpy as jnp
from jax import lax
from jax.experimental import pallas as pl
from jax.experimental.pallas import tpu as pltpu

F32 = jnp.float32
BF16 = jnp.bfloat16

RMS_EPS = 1e-6
N_MOD = 6
GRID_W = 64
ROPE_BASE = 10000.0

V_HEAD = 128
QK_NOPE = 128
QK_ROPE = 64
QK_HEAD = QK_NOPE + QK_ROPE
Q_LORA = 512
KV_LORA = 512
ROPE_AXIS = QK_ROPE // 2
HEAD_PAD = 256

POOL_WINDOWS = (2, 4, 8, 16)
POOL_PAD = 16

LRU_BLOCKS = 8
CONV_W = 4
CONV_PAD = 8
LRU_C = 8.0

N_EXPERTS = 16
EC_FACTOR = 2

SUBLANES = 8
LANES = 128
VMEM_LIMIT_BYTES = 56 * 1024 * 1024


def _params(n_grid):
    return pltpu.CompilerParams(dimension_semantics=("arbitrary",) * n_grid,
                                vmem_limit_bytes=VMEM_LIMIT_BYTES)


def _const_spec(shape):
    nd = len(shape)
    return pl.BlockSpec(shape, lambda *_: (0,) * nd)


def _rms_scale(xf):
    return xf * lax.rsqrt(jnp.mean(xf * xf, axis=-1, keepdims=True) + RMS_EPS)


def _dot(a, b):
    return jnp.dot(a, b, preferred_element_type=F32)


def _row_tile(t, pref):
    return pref if t % pref == 0 else t


def _mods_kernel(c_ref, w_ref, b_ref, o_ref):
    c = c_ref[...]
    s = c * jax.nn.sigmoid(c)
    o_ref[0] = _dot(s.astype(BF16), w_ref[0].astype(BF16)) + b_ref[0]


def _mods(c_rows, w_ada, b_ada):
    depth, d, n = w_ada.shape
    tn = 1024
    return pl.pallas_call(
        _mods_kernel,
        grid=(depth, n // tn),
        in_specs=[_const_spec(c_rows.shape),
                  pl.BlockSpec((1, d, tn), lambda l, j: (l, 0, j)),
                  pl.BlockSpec((1, 1, tn), lambda l, j: (l, 0, j))],
        out_specs=pl.BlockSpec((1, c_rows.shape[0], tn), lambda l, j: (l, 0, j)),
        out_shape=jax.ShapeDtypeStruct((depth, c_rows.shape[0], n), F32),
        compiler_params=_params(2),
        name="adaln_mods",
    )(c_rows, w_ada, b_ada.reshape(depth, 1, n))


def _even_pre_kernel(x_ref, sc_ref, sh_ref, g_ref, win_ref, gqa_ref, gkv_ref, wq_ref, wk_ref, wv_ref, cs_ref,
                     q_ref, k_ref, v_ref, u_ref, *, n_heads, scale, pool_w):
    h = _rms_scale(x_ref[0]) * (g_ref[...] * (1.0 + sc_ref[0])) + sh_ref[0]
    p = _dot(h.astype(BF16), win_ref[...])
    cs = cs_ref[...]
    o_pool = Q_LORA + KV_LORA
    o_rope = o_pool + pool_w

    ql = _rms_scale(p[:, :Q_LORA]) * gqa_ref[...]
    q = _dot(ql.astype(BF16), wq_ref[...])
    for hd in range(n_heads):
        base = hd * HEAD_PAD
        hi = q[:, base + QK_NOPE:base + HEAD_PAD] * cs
        hi = hi + pltpu.roll(hi, QK_ROPE, axis=1)
        q_ref[0, :, base:base + QK_NOPE] = (q[:, base:base + QK_NOPE] * scale).astype(BF16)
        q_ref[0, :, base + QK_NOPE:base + HEAD_PAD] = (hi * scale).astype(BF16)

    kvl = (_rms_scale(p[:, Q_LORA:o_pool]) * gkv_ref[...]).astype(BF16)
    kn = _dot(kvl, wk_ref[...])
    v_ref[0] = _dot(kvl, wv_ref[...]).astype(BF16)
    kp = p[:, o_rope:o_rope + 2 * QK_ROPE] * cs
    kp = kp + pltpu.roll(kp, QK_ROPE, axis=1)
    lane = lax.broadcasted_iota(jnp.int32, kp.shape, 1)
    kp = jnp.where(lane < QK_ROPE, kp, 0.0).astype(BF16)
    for hd in range(n_heads):
        base = hd * HEAD_PAD
        k_ref[0, :, base:base + QK_NOPE] = kn[:, hd * QK_NOPE:(hd + 1) * QK_NOPE].astype(BF16)
        k_ref[0, :, base + QK_NOPE:base + HEAD_PAD] = kp
    u_ref[0] = p[:, o_pool:o_rope]


def _even_pre(x, sc, sh, g, win, gqa, gkv, wq, wk, wv, cs_tab, n_heads):
    b, t, d = x.shape
    tm = _row_tile(t, 256)
    pool_w = d // 2
    kern = functools.partial(_even_pre_kernel, n_heads=n_heads, scale=QK_HEAD ** -0.5, pool_w=pool_w)
    tok = lambda w: pl.BlockSpec((1, tm, w), lambda bi, i: (bi, i, 0))
    per_b = pl.BlockSpec((1, 1, d), lambda bi, i: (bi, 0, 0))
    return pl.pallas_call(
        kern,
        grid=(b, t // tm),
        in_specs=[tok(d), per_b, per_b, _const_spec(g.shape), _const_spec(win.shape), _const_spec(gqa.shape),
                  _const_spec(gkv.shape), _const_spec(wq.shape), _const_spec(wk.shape), _const_spec(wv.shape),
                  pl.BlockSpec((tm, 2 * QK_ROPE), lambda bi, i: (i, 0))],
        out_specs=[tok(n_heads * HEAD_PAD), tok(n_heads * HEAD_PAD), tok(n_heads * V_HEAD), tok(pool_w)],
        out_shape=[jax.ShapeDtypeStruct((b, t, n_heads * HEAD_PAD), BF16),
                   jax.ShapeDtypeStruct((b, t, n_heads * HEAD_PAD), BF16),
                   jax.ShapeDtypeStruct((b, t, n_heads * V_HEAD), BF16),
                   jax.ShapeDtypeStruct((b, t, pool_w), F32)],
        compiler_params=_params(2),
        name="even_pre",
    )(x, sc, sh, g, win, gqa, gkv, wq, wk, wv, cs_tab)


def _attn_kernel(*refs, n_kv):
    q_ref, o_ref = refs[0], refs[-1]
    q = q_ref[0]
    nt = (((1,), (1,)), ((), ()))
    ss = [lax.dot_general(q, refs[1 + 2 * i][0], nt, preferred_element_type=F32) for i in range(n_kv)]
    m = functools.reduce(jnp.maximum, [jnp.max(s, axis=-1, keepdims=True) for s in ss])
    ps = [jnp.exp(s - m) for s in ss]
    l = functools.reduce(jnp.add, [jnp.sum(p, axis=-1, keepdims=True) for p in ps])
    o = functools.reduce(jnp.add, [_dot(ps[i].astype(BF16), refs[2 + 2 * i][0]) for i in range(n_kv)])
    o_ref[0] = (o * (1.0 / l)).astype(BF16)


def _attention(q, kvs, n_heads):
    b, tq_all, _ = q.shape
    tq = _row_tile(tq_all, 512)
    in_specs = [pl.BlockSpec((1, tq, HEAD_PAD), lambda bi, hd, i: (bi, i, hd))]
    args = [q]
    for k, v in kvs:
        s = k.shape[1]
        in_specs.append(pl.BlockSpec((1, s, HEAD_PAD), lambda bi, hd, i: (bi, 0, hd)))
        in_specs.append(pl.BlockSpec((1, s, V_HEAD), lambda bi, hd, i: (bi, 0, hd)))
        args += [k, v]
    return pl.pallas_call(
        functools.partial(_attn_kernel, n_kv=len(kvs)),
        grid=(b, n_heads, tq_all // tq),
        in_specs=in_specs,
        out_specs=pl.BlockSpec((1, tq, V_HEAD), lambda bi, hd, i: (bi, i, hd)),
        out_shape=jax.ShapeDtypeStruct((b, tq_all, n_heads * V_HEAD), BF16),
        compiler_params=_params(3),
        name="mla_attention",
    )(*args)


def _pool_kernel(u_ref, w_ref, ps_ref, o_ref, e0, e1, *, t, windows):
    gi = pl.program_id(1)
    gc = u_ref.shape[-1]
    u = u_ref[0]
    zpad = jnp.zeros((POOL_PAD, gc), F32)
    for e in (e0, e1):
        e[0:POOL_PAD, :] = zpad
        e[POOL_PAD + t:2 * POOL_PAD + t, :] = zpad
    e0[POOL_PAD:POOL_PAD + t, :] = u
    pos = lax.broadcasted_iota(jnp.int32, (t, 1), 0)
    n = t + 2 * POOL_PAD - 2 * SUBLANES
    for k, w in enumerate(windows):
        @pl.when(gi == k)
        def _(w=w):
            src, dst = e0, e1
            dst[SUBLANES:SUBLANES + n, :] = src[SUBLANES - 1:SUBLANES - 1 + n, :] + src[SUBLANES:SUBLANES + n, :]
            src, dst = dst, src
            m = 2
            while m < w:
                hm = m // 2
                dst[SUBLANES:SUBLANES + n, :] = (src[SUBLANES - hm:SUBLANES - hm + n, :]
                                                 + src[SUBLANES + hm:SUBLANES + hm + n, :])
                src, dst = dst, src
                m *= 2
            tot = src[POOL_PAD:POOL_PAD + t, :]
            cnt = (jnp.minimum(pos + (w - w // 2), t) - jnp.maximum(pos - w // 2, 0)).astype(F32)
            pooled = tot / cnt - u
            o_ref[0] = (_dot(pooled.astype(BF16), w_ref[0]) * ps_ref[...]).astype(BF16)


def _pool(u, w_pool, pool_scale):
    b, t, pw = u.shape
    ng = len(POOL_WINDOWS)
    gc = pw // ng
    return pl.pallas_call(
        functools.partial(_pool_kernel, t=t, windows=POOL_WINDOWS),
        grid=(b, ng),
        in_specs=[pl.BlockSpec((1, t, gc), lambda bi, g: (bi, 0, g)),
                  pl.BlockSpec((1, gc, gc), lambda bi, g: (g, 0, 0)),
                  pl.BlockSpec((1, gc), lambda bi, g: (0, g))],
        out_specs=pl.BlockSpec((1, t, gc), lambda bi, g: (bi, 0, g)),
        out_shape=jax.ShapeDtypeStruct((b, t, pw), BF16),
        scratch_shapes=[pltpu.VMEM((t + 2 * POOL_PAD, gc), F32), pltpu.VMEM((t + 2 * POOL_PAD, gc), F32)],
        compiler_params=_params(2),
        name="pool_mixer",
    )(u, w_pool, pool_scale)


def _mix_out_kernel(*refs, n_a):
    a_refs, w_refs = refs[:n_a], refs[n_a:2 * n_a]
    x_ref, gt_ref, g2_ref, sc_ref, sh_ref, wr_ref, xo_ref, h_ref, lg_ref = refs[2 * n_a:]
    m = functools.reduce(jnp.add, [_dot(a_refs[i][0], w_refs[i][...]) for i in range(n_a)])
    xn = x_ref[0] + gt_ref[0] * m
    xo_ref[0] = xn
    hf = _rms_scale(xn) * (g2_ref[...] * (1.0 + sc_ref[0])) + sh_ref[0]
    hb = hf.astype(BF16)
    h_ref[0] = hb
    h_lo = (hf - hb.astype(F32)).astype(BF16)
    wr = wr_ref[...]
    w_hi = wr.astype(BF16)
    w_lo = (wr - w_hi.astype(F32)).astype(BF16)
    lg_ref[0] = _dot(hb, w_hi) + (_dot(hb, w_lo) + _dot(h_lo, w_hi))


def _mix_out(a_list, w_list, x, gt, g2, sc, sh, w_router):
    b, t, d = x.shape
    tm = _row_tile(t, 256)
    ne = w_router.shape[-1]
    tok = lambda w: pl.BlockSpec((1, tm, w), lambda bi, i: (bi, i, 0))
    per_b = pl.BlockSpec((1, 1, d), lambda bi, i: (bi, 0, 0))
    in_specs = ([tok(a.shape[-1]) for a in a_list] + [_const_spec(w.shape) for w in w_list]
                + [tok(d), per_b, _const_spec(g2.shape), per_b, per_b, _const_spec(w_router.shape)])
    return pl.pallas_call(
        functools.partial(_mix_out_kernel, n_a=len(a_list)),
        grid=(b, t // tm),
        in_specs=in_specs,
        out_specs=[tok(d), tok(d), tok(ne)],
        out_shape=[jax.ShapeDtypeStruct((b, t, d), F32), jax.ShapeDtypeStruct((b, t, d), BF16),
                   jax.ShapeDtypeStruct((b, t, ne), F32)],
        compiler_params=_params(2),
        name="mix_out",
    )(*a_list, *w_list, x, gt, g2, sc, sh, w_router)


def _norm_matmul_kernel(x_ref, sc_ref, sh_ref, g_ref, w_ref, o_ref):
    h = _rms_scale(x_ref[0]) * (g_ref[...] * (1.0 + sc_ref[0])) + sh_ref[0]
    o_ref[0] = _dot(h.astype(BF16), w_ref[...])


def _norm_matmul(x, sc, sh, g, w):
    b, t, d = x.shape
    n = w.shape[1]
    tm = _row_tile(t, 256)
    tn = _row_tile(n, 2048)
    return pl.pallas_call(
        _norm_matmul_kernel,
        grid=(n // tn, b, t // tm),
        in_specs=[pl.BlockSpec((1, tm, d), lambda j, bi, i: (bi, i, 0)),
                  pl.BlockSpec((1, 1, d), lambda j, bi, i: (bi, 0, 0)),
                  pl.BlockSpec((1, 1, d), lambda j, bi, i: (bi, 0, 0)),
                  _const_spec(g.shape),
                  pl.BlockSpec((d, tn), lambda j, bi, i: (0, j))],
        out_specs=pl.BlockSpec((1, tm, tn), lambda j, bi, i: (bi, i, j)),
        out_shape=jax.ShapeDtypeStruct((b, t, n), F32),
        compiler_params=_params(3),
        name="norm_matmul",
    )(x, sc, sh, g, w)


def _rglru_kernel(gate_ref, rec_ref, recc_ref, cw_ref, cb_ref, wa_ref, ba_ref, wx_ref, bx_ref, lam_ref, o_ref,
                  ext, extc, abuf, bbuf, ybuf, *, t, tc, chunk):
    c = rec_ref.shape[-1]
    zpad = jnp.zeros((CONV_PAD, c), F32)
    for e, r, n in ((ext, rec_ref, t), (extc, recc_ref, tc)):
        e[0:CONV_PAD, :] = zpad
        e[CONV_PAD + n:2 * CONV_PAD + n, :] = zpad
        e[CONV_PAD:CONV_PAD + n, :] = r[0]
    cw = cw_ref[...]
    cb = cb_ref[...]
    row = lax.broadcasted_iota(jnp.int32, (SUBLANES, c), 0)

    def gates(e_ref, r0, n, d):
        u = cb
        for k in range(CONV_W):
            off = CONV_PAD + r0 + k - CONV_W // 2
            u = u + cw[k:k + 1] * e_ref[off:off + n, :]
        ub = u.astype(BF16)
        r = jax.nn.sigmoid(_dot(ub, wa_ref[d, 0]) + ba_ref[d:d + 1])
        i = jax.nn.sigmoid(_dot(ub, wx_ref[d, 0]) + bx_ref[d:d + 1])
        log_a = (-LRU_C * r) * jax.nn.softplus(-lam_ref[d:d + 1])
        a = jnp.exp(log_a)
        abuf[0:n, :] = a
        bbuf[0:n, :] = jnp.sqrt(1.0 - a * a) * (i * u)

    def scan(n, carry, d, emit):
        nb = n // SUBLANES

        def body(j, hc):
            jj = j if d == 0 else nb - 1 - j
            off = pl.multiple_of(jj * SUBLANES, SUBLANES)
            a = abuf[pl.ds(off, SUBLANES), :]
            bb = bbuf[pl.ds(off, SUBLANES), :]
            for s in (1, 2, 4):
                if d == 0:
                    sh, msk = s, row >= s
                else:
                    sh, msk = SUBLANES - s, row < SUBLANES - s
                a_s = jnp.where(msk, pltpu.roll(a, sh, axis=0), 1.0)
                b_s = jnp.where(msk, pltpu.roll(bb, sh, axis=0), 0.0)
                bb = a * b_s + bb
                a = a * a_s
            h = a * hc + bb
            emit(off, h)
            return h[SUBLANES - 1:SUBLANES] if d == 0 else h[0:1]

        return lax.fori_loop(0, nb, body, carry)

    def run(e_ref, n_all, d, carry, emit):
        cn = min(chunk, n_all)
        starts = list(range(0, n_all, cn))
        for r0 in (starts if d == 0 else starts[::-1]):
            gates(e_ref, r0, cn, d)
            carry = scan(cn, carry, d, functools.partial(emit, r0))
        return carry

    def emit_none(r0, off, h):
        pass

    def emit_set(r0, off, h):
        ybuf[pl.ds(r0 + off, SUBLANES), :] = h

    def emit_add(r0, off, h):
        ybuf[pl.ds(r0 + off, SUBLANES), :] += h

    h0 = jnp.zeros((1, c), F32)
    run(ext, t, 0, run(extc, tc, 0, h0, emit_none), emit_set)
    run(ext, t, 1, run(extc, tc, 1, h0, emit_none), emit_add)
    cn = min(chunk, t)
    for r0 in range(0, t, cn):
        o_ref[0, r0:r0 + cn, :] = (ybuf[r0:r0 + cn, :] * jax.nn.gelu(gate_ref[0, r0:r0 + cn, :])).astype(BF16)


def _rglru(p_lat, p_ctx, conv_w, conv_b, w_a, b_a, w_x, b_x, lam):
    b, t, c2 = p_lat.shape
    cw = c2 // 2
    tc = p_ctx.shape[1]
    nb = w_a.shape[1]
    bs = cw // nb
    chunk = 512
    vec = lambda rows: pl.BlockSpec((rows, bs), lambda bi, n: (0, n))
    wspec = pl.BlockSpec((2, 1, bs, bs), lambda bi, n: (0, n, 0, 0))
    return pl.pallas_call(
        functools.partial(_rglru_kernel, t=t, tc=tc, chunk=chunk),
        grid=(b, nb),
        in_specs=[pl.BlockSpec((1, t, bs), lambda bi, n: (bi, 0, n)),
                  pl.BlockSpec((1, t, bs), lambda bi, n: (bi, 0, nb + n)),
                  pl.BlockSpec((1, tc, bs), lambda bi, n: (bi, 0, n)),
                  vec(CONV_W), vec(1), wspec, vec(2), wspec, vec(2), vec(2)],
        out_specs=pl.BlockSpec((1, t, bs), lambda bi, n: (bi, 0, n)),
        out_shape=jax.ShapeDtypeStruct((b, t, cw), BF16),
        scratch_shapes=[pltpu.VMEM((t + 2 * CONV_PAD, bs), F32), pltpu.VMEM((tc + 2 * CONV_PAD, bs), F32),
                        pltpu.VMEM((min(chunk, max(t, tc)), bs), F32), pltpu.VMEM((min(chunk, max(t, tc)), bs), F32),
                        pltpu.VMEM((t, bs), F32)],
        compiler_params=_params(2),
        name="rglru",
    )(p_lat, p_lat, p_ctx, conv_w, conv_b.reshape(1, cw), w_a, b_a, w_x, b_x, lam)


def _moe_ffn_kernel(xs_ref, g_ref, w1_ref, w3_ref, w2_ref, o_ref, acc):
    f = pl.program_id(2)

    @pl.when(f == 0)
    def _():
        acc[...] = jnp.zeros_like(acc)

    x = xs_ref[0]
    a1 = _dot(x, w1_ref[0].astype(BF16))
    a3 = _dot(x, w3_ref[0].astype(BF16))
    hmid = (a1 * jax.nn.sigmoid(a1) * a3).astype(BF16)
    acc[...] += _dot(hmid, w2_ref[0].astype(BF16))

    @pl.when(f == pl.num_programs(2) - 1)
    def _():
        o_ref[0] = (acc[...] * g_ref[0]).astype(o_ref.dtype)


def _moe_ffn(xs, gates, w1, w3, w2):
    e, r, d = xs.shape
    ff = w1.shape[-1]
    tr = r // 2 if (r // 2) % SUBLANES == 0 else r
    tf = _row_tile(ff, 256)
    return pl.pallas_call(
        _moe_ffn_kernel,
        grid=(e, r // tr, ff // tf),
        in_specs=[pl.BlockSpec((1, tr, d), lambda ei, ri, f: (ei, ri, 0)),
                  pl.BlockSpec((1, tr, 1), lambda ei, ri, f: (ei, ri, 0)),
                  pl.BlockSpec((1, d, tf), lambda ei, ri, f: (ei, 0, f)),
                  pl.BlockSpec((1, d, tf), lambda ei, ri, f: (ei, 0, f)),
                  pl.BlockSpec((1, tf, d), lambda ei, ri, f: (ei, f, 0))],
        out_specs=pl.BlockSpec((1, tr, d), lambda ei, ri, f: (ei, ri, 0)),
        out_shape=jax.ShapeDtypeStruct((e, r, d), F32),
        scratch_shapes=[pltpu.VMEM((tr, d), F32)],
        compiler_params=_params(3),
        name="moe_ffn",
    )(xs, gates, w1, w3, w2)


def _final_norm_kernel(x_ref, g_ref, o_ref):
    o_ref[0] = _rms_scale(x_ref[0]) * g_ref[...]


def _final_norm(x, g):
    b, t, d = x.shape
    tm = _row_tile(t, 512)
    return pl.pallas_call(
        _final_norm_kernel,
        grid=(b, t // tm),
        in_specs=[pl.BlockSpec((1, tm, d), lambda bi, i: (bi, i, 0)), _const_spec(g.shape)],
        out_specs=pl.BlockSpec((1, tm, d), lambda bi, i: (bi, i, 0)),
        out_shape=jax.ShapeDtypeStruct((b, t, d), F32),
        compiler_params=_params(2),
        name="final_norm",
    )(x, g)


def _rope_tables(t):
    rows = t // GRID_W
    row = jnp.repeat(jnp.arange(rows, dtype=F32), GRID_W)
    col = jnp.tile(jnp.arange(GRID_W, dtype=F32), rows)
    inv = ROPE_BASE ** (-jnp.arange(0, ROPE_AXIS, 2, dtype=F32) / ROPE_AXIS)
    ar = row[:, None] * inv
    ac = col[:, None] * inv
    cos = jnp.concatenate([jnp.cos(ar), jnp.cos(ar), jnp.cos(ac), jnp.cos(ac)], axis=-1)
    sin = jnp.concatenate([jnp.sin(ar), jnp.sin(ar), jnp.sin(ac), jnp.sin(ac)], axis=-1)
    return jnp.concatenate([cos, sin], axis=-1)


def _rot_columns(w):
    half = ROPE_AXIS // 2
    return jnp.concatenate([-w[..., half:ROPE_AXIS], w[..., :half],
                            -w[..., ROPE_AXIS + half:], w[..., ROPE_AXIS:ROPE_AXIS + half]], axis=-1)


def _route(logits, cap):
    aff = jax.nn.softmax(logits, axis=-1)
    return lax.top_k(jnp.swapaxes(aff, 1, 2), cap)


def _moe(streams, w1, w3, w2):
    ne = w1.shape[0]
    xs_parts, g_parts, idxs = [], [], []
    for h, logits in streams:
        b, t, d = h.shape
        cap = EC_FACTOR * t // ne
        g, idx = _route(logits, cap)
        xs = jax.vmap(lambda hb, ib: hb[ib])(h, idx)
        xs_parts.append(jnp.swapaxes(xs, 0, 1).reshape(ne, b * cap, d))
        g_parts.append(jnp.swapaxes(g, 0, 1).reshape(ne, b * cap, 1))
        idxs.append(idx)
    y = _moe_ffn(jnp.concatenate(xs_parts, axis=1), jnp.concatenate(g_parts, axis=1), w1, w3, w2)
    outs, r0 = [], 0
    for (h, _), idx in zip(streams, idxs):
        b, t, d = h.shape
        cap = idx.shape[-1]
        yb = jnp.swapaxes(y[:, r0:r0 + b * cap].reshape(ne, b, cap, d), 0, 1)
        outs.append(jax.vmap(lambda ib, ybb: jnp.zeros((t, d), F32).at[ib.reshape(-1)].add(ybb.reshape(-1, d)))(idx, yb))
        r0 += b * cap
    return outs


def kernel(x, c, ctx, c_ctx, w_ada, b_ada, norm_g, final_g, w_in_e, g_qa, w_uq, g_kv, w_ukv, w_pool, pool_scale,
           w_out_e, w_in_o, conv_w, conv_b, w_rg_a, b_rg_a, w_rg_x, b_rg_x, lru_lambda, w_out_o, w_router, w_e1,
           w_e3, w_e2):
    b, t, d = x.shape
    tc = ctx.shape[1]
    depth = w_ada.shape[0]
    n_heads = d // (2 * V_HEAD)
    pool_w = d // 2

    c_rows = jnp.concatenate([c, c_ctx[None], jnp.zeros((SUBLANES - b - 1, d), F32)], axis=0)
    mods = _mods(c_rows, w_ada, b_ada)

    def mod_rows(layer, k):
        m = mods[layer, :, k * d:(k + 1) * d]
        return m[:b, None, :], jnp.broadcast_to(m[b][None, None, :], (b, 1, d))

    cs_lat = _rope_tables(t)
    cs_ctx = jnp.concatenate([jnp.ones((tc, QK_ROPE), F32), jnp.zeros((tc, QK_ROPE), F32)], axis=-1)

    cs = ctx
    for layer in range(depth):
        last = layer == depth - 1
        (sh_m, csh_m), (sc_m, csc_m), (gt_m, cgt_m), (sh_f, csh_f), (sc_f, csc_f), (gt_f, cgt_f) = (
            mod_rows(layer, k) for k in range(N_MOD))
        g1 = norm_g[layer, 0][None]
        g2 = norm_g[layer, 1][None]
        if layer % 2 == 0:
            e = layer // 2
            wi = w_in_e[e]
            o_kpe = Q_LORA + KV_LORA
            kpe = wi[:, o_kpe:o_kpe + QK_ROPE]
            win = jnp.concatenate([wi[:, :o_kpe], wi[:, o_kpe + QK_ROPE:], kpe, _rot_columns(kpe)], axis=1).astype(BF16)
            wq3 = w_uq[e].reshape(Q_LORA, n_heads, QK_HEAD)
            wq = jnp.concatenate([wq3, _rot_columns(wq3[..., QK_NOPE:])], axis=-1).reshape(Q_LORA, n_heads * HEAD_PAD)
            wkv3 = w_ukv[e].reshape(KV_LORA, n_heads, QK_NOPE + V_HEAD)
            wk = wkv3[..., :QK_NOPE].reshape(KV_LORA, n_heads * QK_NOPE).astype(BF16)
            wv = wkv3[..., QK_NOPE:].reshape(KV_LORA, n_heads * V_HEAD).astype(BF16)
            pre = functools.partial(_even_pre, g=g1, win=win, gqa=g_qa[e][None], gkv=g_kv[e][None],
                                    wq=wq.astype(BF16), wk=wk, wv=wv, n_heads=n_heads)
            q_l, k_l, v_l, u_l = pre(x, sc_m, sh_m, cs_tab=cs_lat)
            q_c, k_c, v_c, u_c = pre(cs, csc_m, csh_m, cs_tab=cs_ctx)
            wo = w_out_e[e].astype(BF16)
            wo_a, wo_p = wo[:n_heads * V_HEAD], wo[n_heads * V_HEAD:]
            a_l = _attention(q_l, [(k_c, v_c), (k_l, v_l)], n_heads)
            pl_l = _pool(u_l, w_pool[e].astype(BF16), pool_scale[e][None])
            x, h_l, lg_l = _mix_out([a_l, pl_l], [wo_a, wo_p], x, gt_m, g2, sc_f, sh_f, w_router[layer])
            streams = [(h_l, lg_l)]
            if not last:
                a_c = _attention(q_c, [(k_c, v_c)], n_heads)
                pl_c = _pool(u_c, w_pool[e].astype(BF16), pool_scale[e][None])
                cs, h_c, lg_c = _mix_out([a_c, pl_c], [wo_a, wo_p], cs, cgt_m, g2, csc_f, csh_f, w_router[layer])
                streams.append((h_c, lg_c))
        else:
            o = layer // 2
            wi = w_in_o[o].astype(BF16)
            lw = wi.shape[1] // 2
            p_l = _norm_matmul(x, sc_m, sh_m, g1, wi)
            p_c = _norm_matmul(cs, csc_m, csh_m, g1, wi if not last else wi[:, lw:])
            wo = w_out_o[o].astype(BF16)
            y_l = _rglru(p_l, p_c[..., p_c.shape[-1] - lw:], conv_w[o], conv_b[o], w_rg_a[o].astype(BF16), b_rg_a[o],
                         w_rg_x[o].astype(BF16), b_rg_x[o], lru_lambda[o])
            x, h_l, lg_l = _mix_out([y_l], [wo], x, gt_m, g2, sc_f, sh_f, w_router[layer])
            streams = [(h_l, lg_l)]
            if not last:
                raise NotImplementedError("context output of an RG-LRU layer is only needed when it is not the last layer")
        ys = _moe(streams, w_e1[layer], w_e3[layer], w_e2[layer])
        x = x + gt_f * ys[0]
        if not last:
            cs = cs + cgt_f * ys[1]
    return _final_norm(x, final_g[None])
```

```python
import functools
import math

import jax
import jax.numpy as jnp
from jax import lax
from jax.experimental import pallas as pl
from jax.experimental.pallas import tpu as pltpu

F32 = jnp.float32
BF16 = jnp.bfloat16

RMS_EPS = 1e-6
N_MOD = 6
GRID_W = 64
ROPE_BASE = 10000.0

V_HEAD = 128
QK_NOPE = 128
QK_ROPE = 64
QK_HEAD = QK_NOPE + QK_ROPE
Q_LORA = 512
KV_LORA = 512
ROPE_AXIS = QK_ROPE // 2
HEAD_PAD = 256

POOL_WINDOWS = (2, 4, 8, 16)
POOL_PAD = 16

LRU_BLOCKS = 8
CONV_W = 4
CONV_PAD = 8
LRU_C = 8.0

N_EXPERTS = 16
EC_FACTOR = 2

SUBLANES = 8
LANES = 128
VMEM_LIMIT_BYTES = 56 * 1024 * 1024


def _params(n_grid):
    return pltpu.CompilerParams(dimension_semantics=("arbitrary",) * n_grid,
                                vmem_limit_bytes=VMEM_LIMIT_BYTES)


def _const_spec(shape):
    nd = len(shape)
    return pl.BlockSpec(shape, lambda *_: (0,) * nd)


def _rms_scale(xf):
    return xf * lax.rsqrt(jnp.mean(xf * xf, axis=-1, keepdims=True) + RMS_EPS)


def _dot(a, b):
    return jnp.dot(a, b, preferred_element_type=F32)


def _row_tile(t, pref):
    return pref if t % pref == 0 else t


def _mods_kernel(c_ref, w_ref, b_ref, o_ref):
    c = c_ref[...]
    s = c * jax.nn.sigmoid(c)
    o_ref[0] = _dot(s.astype(BF16), w_ref[0].astype(BF16)) + b_ref[0]


def _mods(c_rows, w_ada, b_ada):
    depth, d, n = w_ada.shape
    tn = 1024
    return pl.pallas_call(
        _mods_kernel,
        grid=(depth, n // tn),
        in_specs=[_const_spec(c_rows.shape),
                  pl.BlockSpec((1, d, tn), lambda l, j: (l, 0, j)),
                  pl.BlockSpec((1, 1, tn), lambda l, j: (l, 0, j))],
        out_specs=pl.BlockSpec((1, c_rows.shape[0], tn), lambda l, j: (l, 0, j)),
        out_shape=jax.ShapeDtypeStruct((depth, c_rows.shape[0], n), F32),
        compiler_params=_params(2),
        name="adaln_mods",
    )(c_rows, w_ada, b_ada.reshape(depth, 1, n))


def _even_pre_kernel(x_ref, sc_ref, sh_ref, g_ref, win_ref, gqa_ref, gkv_ref, wq_ref, wk_ref, wv_ref, cs_ref,
                     q_ref, k_ref, v_ref, u_ref, *, n_heads, scale, pool_w):
    h = _rms_scale(x_ref[0]) * (g_ref[...] * (1.0 + sc_ref[0])) + sh_ref[0]
    p = _dot(h.astype(BF16), win_ref[...])
    cs = cs_ref[...]
    o_pool = Q_LORA + KV_LORA
    o_rope = o_pool + pool_w

    ql = _rms_scale(p[:, :Q_LORA]) * gqa_ref[...]
    q = _dot(ql.astype(BF16), wq_ref[...])
    for hd in range(n_heads):
        base = hd * HEAD_PAD
        hi = q[:, base + QK_NOPE:base + HEAD_PAD] * cs
        hi = hi + pltpu.roll(hi, QK_ROPE, axis=1)
        q_ref[0, :, base:base + QK_NOPE] = (q[:, base:base + QK_NOPE] * scale).astype(BF16)
        q_ref[0, :, base + QK_NOPE:base + HEAD_PAD] = (hi * scale).astype(BF16)

    kvl = (_rms_scale(p[:, Q_LORA:o_pool]) * gkv_ref[...]).astype(BF16)
    kn = _dot(kvl, wk_ref[...])
    v_ref[0] = _dot(kvl, wv_ref[...]).astype(BF16)
    kp = p[:, o_rope:o_rope + 2 * QK_ROPE] * cs
    kp = kp + pltpu.roll(kp, QK_ROPE, axis=1)
    lane = lax.broadcasted_iota(jnp.int32, kp.shape, 1)
    kp = jnp.where(lane < QK_ROPE, kp, 0.0).astype(BF16)
    for hd in range(n_heads):
        base = hd * HEAD_PAD
        k_ref[0, :, base:base + QK_NOPE] = kn[:, hd * QK_NOPE:(hd + 1) * QK_NOPE].astype(BF16)
        k_ref[0, :, base + QK_NOPE:base + HEAD_PAD] = kp
    u_ref[0] = p[:, o_pool:o_rope]


def _even_pre(x, sc, sh, g, win, gqa, gkv, wq, wk, wv, cs_tab, n_heads):
    b, t, d = x.shape
    tm = _row_tile(t, 256)
    pool_w = d // 2
    kern = functools.partial(_even_pre_kernel, n_heads=n_heads, scale=QK_HEAD ** -0.5, pool_w=pool_w)
    tok = lambda w: pl.BlockSpec((1, tm, w), lambda bi, i: (bi, i, 0))
    per_b = pl.BlockSpec((1, 1, d), lambda bi, i: (bi, 0, 0))
    return pl.pallas_call(
        kern,
        grid=(b, t // tm),
        in_specs=[tok(d), per_b, per_b, _const_spec(g.shape), _const_spec(win.shape), _const_spec(gqa.shape),
                  _const_spec(gkv.shape), _const_spec(wq.shape), _const_spec(wk.shape), _const_spec(wv.shape),
                  pl.BlockSpec((tm, 2 * QK_ROPE), lambda bi, i: (i, 0))],
        out_specs=[tok(n_heads * HEAD_PAD), tok(n_heads * HEAD_PAD), tok(n_heads * V_HEAD), tok(pool_w)],
        out_shape=[jax.ShapeDtypeStruct((b, t, n_heads * HEAD_PAD), BF16),
                   jax.ShapeDtypeStruct((b, t, n_heads * HEAD_PAD), BF16),
                   jax.ShapeDtypeStruct((b, t, n_heads * V_HEAD), BF16),
                   jax.ShapeDtypeStruct((b, t, pool_w), F32)],
        compiler_params=_params(2),
        name="even_pre",
    )(x, sc, sh, g, win, gqa, gkv, wq, wk, wv, cs_tab)


def _attn_kernel(*refs, n_kv):
    q_ref, o_ref = refs[0], refs[-1]
    q = q_ref[0]
    nt = (((1,), (1,)), ((), ()))
    ss = [lax.dot_general(q, refs[1 + 2 * i][0], nt, preferred_element_type=F32) for i in range(n_kv)]
    m = functools.reduce(jnp.maximum, [jnp.max(s, axis=-1, keepdims=True) for s in ss])
    ps = [jnp.exp(s - m) for s in ss]
    l = functools.reduce(jnp.add, [jnp.sum(p, axis=-1, keepdims=True) for p in ps])
    o = functools.reduce(jnp.add, [_dot(ps[i].astype(BF16), refs[2 + 2 * i][0]) for i in range(n_kv)])
    o_ref[0] = (o * (1.0 / l)).astype(BF16)


def _attention(q, kvs, n_heads):
    b, tq_all, _ = q.shape
    tq = _row_tile(tq_all, 512)
    in_specs = [pl.BlockSpec((1, tq, HEAD_PAD), lambda bi, hd, i: (bi, i, hd))]
    args = [q]
    for k, v in kvs:
        s = k.shape[1]
        in_specs.append(pl.BlockSpec((1, s, HEAD_PAD), lambda bi, hd, i: (bi, 0, hd)))
        in_specs.append(pl.BlockSpec((1, s, V_HEAD), lambda bi, hd, i: (bi, 0, hd)))
        args += [k, v]
    return pl.pallas_call(
        functools.partial(_attn_kernel, n_kv=len(kvs)),
        grid=(b, n_heads, tq_all // tq),
        in_specs=in_specs,
        out_specs=pl.BlockSpec((1, tq, V_HEAD), lambda bi, hd, i: (bi, i, hd)),
        out_shape=jax.ShapeDtypeStruct((b, tq_all, n_heads * V_HEAD), BF16),
        compiler_params=_params(3),
        name="mla_attention",
    )(*args)


def _pool_kernel(u_ref, w_ref, ps_ref, o_ref, e0, e1, *, t, windows):
    gi = pl.program_id(1)
    gc = u_ref.shape[-1]
    u = u_ref[0]
    zpad = jnp.zeros((POOL_PAD, gc), F32)
    for e in (e0, e1):
        e[0:POOL_PAD, :] = zpad
        e[POOL_PAD + t:2 * POOL_PAD + t, :] = zpad
    e0[POOL_PAD:POOL_PAD + t, :] = u
    pos = lax.broadcasted_iota(jnp.int32, (t, 1), 0)
    n = t + 2 * POOL_PAD - 2 * SUBLANES
    for k, w in enumerate(windows):
        @pl.when(gi == k)
        def _(w=w):
            src, dst = e0, e1
            dst[SUBLANES:SUBLANES + n, :] = src[SUBLANES - 1:SUBLANES - 1 + n, :] + src[SUBLANES:SUBLANES + n, :]
            src, dst = dst, src
            m = 2
            while m < w:
                hm = m // 2
                dst[SUBLANES:SUBLANES + n, :] = (src[SUBLANES - hm:SUBLANES - hm + n, :]
                                                 + src[SUBLANES + hm:SUBLANES + hm + n, :])
                src, dst = dst, src
                m *= 2
            tot = src[POOL_PAD:POOL_PAD + t, :]
            cnt = (jnp.minimum(pos + (w - w // 2), t) - jnp.maximum(pos - w // 2, 0)).astype(F32)
            pooled = tot / cnt - u
            o_ref[0] = (_dot(pooled.astype(BF16), w_ref[0]) * ps_ref[...]).astype(BF16)


def _pool(u, w_pool, pool_scale):
    b, t, pw = u.shape
    ng = len(POOL_WINDOWS)
    gc = pw // ng
    return pl.pallas_call(
        functools.partial(_pool_kernel, t=t, windows=POOL_WINDOWS),
        grid=(b, ng),
        in_specs=[pl.BlockSpec((1, t, gc), lambda bi, g: (bi, 0, g)),
                  pl.BlockSpec((1, gc, gc), lambda bi, g: (g, 0, 0)),
                  pl.BlockSpec((1, gc), lambda bi, g: (0, g))],
        out_specs=pl.BlockSpec((1, t, gc), lambda bi, g: (bi, 0, g)),
        out_shape=jax.ShapeDtypeStruct((b, t, pw), BF16),
        scratch_shapes=[pltpu.VMEM((t + 2 * POOL_PAD, gc), F32), pltpu.VMEM((t + 2 * POOL_PAD, gc), F32)],
        compiler_params=_params(2),
        name="pool_mixer",
    )(u, w_pool, pool_scale)


def _mix_out_kernel(*refs, n_a):
    a_refs, w_refs = refs[:n_a], refs[n_a:2 * n_a]
    x_ref, gt_ref, g2_ref, sc_ref, sh_ref, wr_ref, xo_ref, h_ref, lg_ref = refs[2 * n_a:]
    m = functools.reduce(jnp.add, [_dot(a_refs[i][0], w_refs[i][...]) for i in range(n_a)])
    xn = x_ref[0] + gt_ref[0] * m
    xo_ref[0] = xn
    hf = _rms_scale(xn) * (g2_ref[...] * (1.0 + sc_ref[0])) + sh_ref[0]
    hb = hf.astype(BF16)
    h_ref[0] = hb
    h_lo = (hf - hb.astype(F32)).astype(BF16)
    wr = wr_ref[...]
    w_hi = wr.astype(BF16)
    w_lo = (wr - w_hi.astype(F32)).astype(BF16)
    lg_ref[0] = _dot(hb, w_hi) + (_dot(hb, w_lo) + _dot(h_lo, w_hi))


def _mix_out(a_list, w_list, x, gt, g2, sc, sh, w_router):
    b, t, d = x.shape
    tm = _row_tile(t, 256)
    ne = w_router.shape[-1]
    tok = lambda w: pl.BlockSpec((1, tm, w), lambda bi, i: (bi, i, 0))
    per_b = pl.BlockSpec((1, 1, d), lambda bi, i: (bi, 0, 0))
    in_specs = ([tok(a.shape[-1]) for a in a_list] + [_const_spec(w.shape) for w in w_list]
                + [tok(d), per_b, _const_spec(g2.shape), per_b, per_b, _const_spec(w_router.shape)])
    return pl.pallas_call(
        functools.partial(_mix_out_kernel, n_a=len(a_list)),
        grid=(b, t // tm),
        in_specs=in_specs,
        out_specs=[tok(d), tok(d), tok(ne)],
        out_shape=[jax.ShapeDtypeStruct((b, t, d), F32), jax.ShapeDtypeStruct((b, t, d), BF16),
                   jax.ShapeDtypeStruct((b, t, ne), F32)],
        compiler_params=_params(2),
        name="mix_out",
    )(*a_list, *w_list, x, gt, g2, sc, sh, w_router)


def _norm_matmul_kernel(x_ref, sc_ref, sh_ref, g_ref, w_ref, o_ref):
    h = _rms_scale(x_ref[0]) * (g_ref[...] * (1.0 + sc_ref[0])) + sh_ref[0]
    o_ref[0] = _dot(h.astype(BF16), w_ref[...])


def _norm_matmul(x, sc, sh, g, w):
    b, t, d = x.shape
    n = w.shape[1]
    tm = _row_tile(t, 256)
    tn = _row_tile(n, 2048)
    return pl.pallas_call(
        _norm_matmul_kernel,
        grid=(n // tn, b, t // tm),
        in_specs=[pl.BlockSpec((1, tm, d), lambda j, bi, i: (bi, i, 0)),
                  pl.BlockSpec((1, 1, d), lambda j, bi, i: (bi, 0, 0)),
                  pl.BlockSpec((1, 1, d), lambda j, bi, i: (bi, 0, 0)),
                  _const_spec(g.shape),
                  pl.BlockSpec((d, tn), lambda j, bi, i: (0, j))],
        out_specs=pl.BlockSpec((1, tm, tn), lambda j, bi, i: (bi, i, j)),
        out_shape=jax.ShapeDtypeStruct((b, t, n), F32),
        compiler_params=_params(3),
        name="norm_matmul",
    )(x, sc, sh, g, w)


def _rglru_kernel(gate_ref, rec_ref, recc_ref, cw_ref, cb_ref, wa_ref, ba_ref, wx_ref, bx_ref, lam_ref, o_ref,
                  ext, extc, abuf, bbuf, ybuf, *, t, tc, chunk):
    c = rec_ref.shape[-1]
    zpad = jnp.zeros((CONV_PAD, c), F32)
    for e, r, n in ((ext, rec_ref, t), (extc, recc_ref, tc)):
        e[0:CONV_PAD, :] = zpad
        e[CONV_PAD + n:2 * CONV_PAD + n, :] = zpad
        e[CONV_PAD:CONV_PAD + n, :] = r[0]
    cw = cw_ref[...]
    cb = cb_ref[...]
    row = lax.broadcasted_iota(jnp.int32, (SUBLANES, c), 0)

    def gates(e_ref, r0, n, d):
        u = cb
        for k in range(CONV_W):
            off = CONV_PAD + r0 + k - CONV_W // 2
            u = u + cw[k:k + 1] * e_ref[off:off + n, :]
        ub = u.astype(BF16)
        r = jax.nn.sigmoid(_dot(ub, wa_ref[d, 0]) + ba_ref[d:d + 1])
        i = jax.nn.sigmoid(_dot(ub, wx_ref[d, 0]) + bx_ref[d:d + 1])
        log_a = (-LRU_C * r) * jax.nn.softplus(-lam_ref[d:d + 1])
        a = jnp.exp(log_a)
        abuf[0:n, :] = a
        bbuf[0:n, :] = jnp.sqrt(1.0 - a * a) * (i * u)

    def scan(n, carry, d, emit):
        nb = n // SUBLANES

        def body(j, hc):
            jj = j if d == 0 else nb - 1 - j
            off = pl.multiple_of(jj * SUBLANES, SUBLANES)
            a = abuf[pl.ds(off, SUBLANES), :]
            bb = bbuf[pl.ds(off, SUBLANES), :]
            for s in (1, 2, 4):
                if d == 0:
                    sh, msk = s, row >= s
                else:
                    sh, msk = SUBLANES - s, row < SUBLANES - s
                a_s = jnp.where(msk, pltpu.roll(a, sh, axis=0), 1.0)
                b_s = jnp.where(msk, pltpu.roll(bb, sh, axis=0), 0.0)
                bb = a * b_s + bb
                a = a * a_s
            h = a * hc + bb
            emit(off, h)
            return h[SUBLANES - 1:SUBLANES] if d == 0 else h[0:1]

        return lax.fori_loop(0, nb, body, carry)

    def run(e_ref, n_all, d, carry, emit):
        cn = min(chunk, n_all)
        starts = list(range(0, n_all, cn))
        for r0 in (starts if d == 0 else starts[::-1]):
            gates(e_ref, r0, cn, d)
            carry = scan(cn, carry, d, functools.partial(emit, r0))
        return carry

    def emit_none(r0, off, h):
        pass

    def emit_set(r0, off, h):
        ybuf[pl.ds(r0 + off, SUBLANES), :] = h

    def emit_add(r0, off, h):
        ybuf[pl.ds(r0 + off, SUBLANES), :] += h

    h0 = jnp.zeros((1, c), F32)
    run(ext, t, 0, run(extc, tc, 0, h0, emit_none), emit_set)
    run(ext, t, 1, run(extc, tc, 1, h0, emit_none), emit_add)
    cn = min(chunk, t)
    for r0 in range(0, t, cn):
        o_ref[0, r0:r0 + cn, :] = (ybuf[r0:r0 + cn, :] * jax.nn.gelu(gate_ref[0, r0:r0 + cn, :])).astype(BF16)


def _rglru(p_lat, p_ctx, conv_w, conv_b, w_a, b_a, w_x, b_x, lam):
    b, t, c2 = p_lat.shape
    cw = c2 // 2
    tc = p_ctx.shape[1]
    nb = w_a.shape[1]
    bs = cw // nb
    chunk = 512
    vec = lambda rows: pl.BlockSpec((rows, bs), lambda bi, n: (0, n))
    wspec = pl.BlockSpec((2, 1, bs, bs), lambda bi, n: (0, n, 0, 0))
    return pl.pallas_call(
        functools.partial(_rglru_kernel, t=t, tc=tc, chunk=chunk),
        grid=(b, nb),
        in_specs=[pl.BlockSpec((1, t, bs), lambda bi, n: (bi, 0, n)),
                  pl.BlockSpec((1, t, bs), lambda bi, n: (bi, 0, nb + n)),
                  pl.BlockSpec((1, tc, bs), lambda bi, n: (bi, 0, n)),
                  vec(CONV_W), vec(1), wspec, vec(2), wspec, vec(2), vec(2)],
        out_specs=pl.BlockSpec((1, t, bs), lambda bi, n: (bi, 0, n)),
        out_shape=jax.ShapeDtypeStruct((b, t, cw), BF16),
        scratch_shapes=[pltpu.VMEM((t + 2 * CONV_PAD, bs), F32), pltpu.VMEM((tc + 2 * CONV_PAD, bs), F32),
                        pltpu.VMEM((min(chunk, max(t, tc)), bs), F32), pltpu.VMEM((min(chunk, max(t, tc)), bs), F32),
                        pltpu.VMEM((t, bs), F32)],
        compiler_params=_params(2),
        name="rglru",
    )(p_lat, p_lat, p_ctx, conv_w, conv_b.reshape(1, cw), w_a, b_a, w_x, b_x, lam)


def _moe_ffn_kernel(xs_ref, g_ref, w1_ref, w3_ref, w2_ref, o_ref, acc):
    f = pl.program_id(2)

    @pl.when(f == 0)
    def _():
        acc[...] = jnp.zeros_like(acc)

    x = xs_ref[0]
    a1 = _dot(x, w1_ref[0, 0].astype(BF16))
    a3 = _dot(x, w3_ref[0, 0].astype(BF16))
    hmid = (a1 * jax.nn.sigmoid(a1) * a3).astype(BF16)
    acc[...] += _dot(hmid, w2_ref[0, 0].astype(BF16))

    @pl.when(f == pl.num_programs(2) - 1)
    def _():
        o_ref[0] = (acc[...] * g_ref[0]).astype(o_ref.dtype)


def _moe_ffn(xs, gates, w1, w3, w2, layer):
    e, r, d = xs.shape
    ff = w1.shape[-1]
    tr = r // 2 if (r // 2) % SUBLANES == 0 else r
    tf = _row_tile(ff, 256)
    return pl.pallas_call(
        _moe_ffn_kernel,
        grid=(e, r // tr, ff // tf),
        in_specs=[pl.BlockSpec((1, tr, d), lambda ei, ri, f: (ei, ri, 0)),
                  pl.BlockSpec((1, tr, 1), lambda ei, ri, f: (ei, ri, 0)),
                  pl.BlockSpec((1, 1, d, tf), lambda ei, ri, f: (layer, ei, 0, f)),
                  pl.BlockSpec((1, 1, d, tf), lambda ei, ri, f: (layer, ei, 0, f)),
                  pl.BlockSpec((1, 1, tf, d), lambda ei, ri, f: (layer, ei, f, 0))],
        out_specs=pl.BlockSpec((1, tr, d), lambda ei, ri, f: (ei, ri, 0)),
        out_shape=jax.ShapeDtypeStruct((e, r, d), F32),
        scratch_shapes=[pltpu.VMEM((tr, d), F32)],
        compiler_params=_params(3),
        name="moe_ffn",
    )(xs, gates, w1, w3, w2)


def _final_norm_kernel(x_ref, g_ref, o_ref):
    o_ref[0] = _rms_scale(x_ref[0]) * g_ref[...]


def _final_norm(x, g):
    b, t, d = x.shape
    tm = _row_tile(t, 512)
    return pl.pallas_call(
        _final_norm_kernel,
        grid=(b, t // tm),
        in_specs=[pl.BlockSpec((1, tm, d), lambda bi, i: (bi, i, 0)), _const_spec(g.shape)],
        out_specs=pl.BlockSpec((1, tm, d), lambda bi, i: (bi, i, 0)),
        out_shape=jax.ShapeDtypeStruct((b, t, d), F32),
        compiler_params=_params(2),
        name="final_norm",
    )(x, g)


def _rope_tables(t):
    rows = t // GRID_W
    row = jnp.repeat(jnp.arange(rows, dtype=F32), GRID_W)
    col = jnp.tile(jnp.arange(GRID_W, dtype=F32), rows)
    inv = ROPE_BASE ** (-jnp.arange(0, ROPE_AXIS, 2, dtype=F32) / ROPE_AXIS)
    ar = row[:, None] * inv
    ac = col[:, None] * inv
    cos = jnp.concatenate([jnp.cos(ar), jnp.cos(ar), jnp.cos(ac), jnp.cos(ac)], axis=-1)
    sin = jnp.concatenate([jnp.sin(ar), jnp.sin(ar), jnp.sin(ac), jnp.sin(ac)], axis=-1)
    return jnp.concatenate([cos, sin], axis=-1)


def _rot_columns(w):
    half = ROPE_AXIS // 2
    return jnp.concatenate([-w[..., half:ROPE_AXIS], w[..., :half],
                            -w[..., ROPE_AXIS + half:], w[..., ROPE_AXIS:ROPE_AXIS + half]], axis=-1)


def _route(logits, cap):
    aff = jax.nn.softmax(logits, axis=-1)
    return lax.top_k(jnp.swapaxes(aff, 1, 2), cap)


def _moe(streams, w1, w3, w2, layer):
    ne = w1.shape[1]
    xs_parts, g_parts, idxs = [], [], []
    for h, logits in streams:
        b, t, d = h.shape
        cap = EC_FACTOR * t // ne
        g, idx = _route(logits, cap)
        xs = jax.vmap(lambda hb, ib: hb[ib])(h, idx)
        xs_parts.append(jnp.swapaxes(xs, 0, 1).reshape(ne, b * cap, d))
        g_parts.append(jnp.swapaxes(g, 0, 1).reshape(ne, b * cap, 1))
        idxs.append(idx)
    y = _moe_ffn(jnp.concatenate(xs_parts, axis=1), jnp.concatenate(g_parts, axis=1), w1, w3, w2, layer)
    outs, r0 = [], 0
    for (h, _), idx in zip(streams, idxs):
        b, t, d = h.shape
        cap = idx.shape[-1]
        yb = jnp.swapaxes(y[:, r0:r0 + b * cap].reshape(ne, b, cap, d), 0, 1)
        outs.append(jax.vmap(lambda ib, ybb: jnp.zeros((t, d), F32).at[ib.reshape(-1)].add(ybb.reshape(-1, d)))(idx, yb))
        r0 += b * cap
    return outs


def kernel(x, c, ctx, c_ctx, w_ada, b_ada, norm_g, final_g, w_in_e, g_qa, w_uq, g_kv, w_ukv, w_pool, pool_scale,
           w_out_e, w_in_o, conv_w, conv_b, w_rg_a, b_rg_a, w_rg_x, b_rg_x, lru_lambda, w_out_o, w_router, w_e1,
           w_e3, w_e2):
    b, t, d = x.shape
    tc = ctx.shape[1]
    depth = w_ada.shape[0]
    n_heads = d // (2 * V_HEAD)
    pool_w = d // 2

    c_rows = jnp.concatenate([c, c_ctx[None], jnp.zeros((SUBLANES - b - 1, d), F32)], axis=0)
    mods = _mods(c_rows, w_ada, b_ada)

    def mod_rows(layer, k):
        m = mods[layer, :, k * d:(k + 1) * d]
        return m[:b, None, :], jnp.broadcast_to(m[b][None, None, :], (b, 1, d))

    cs_lat = _rope_tables(t)
    cs_ctx = jnp.concatenate([jnp.ones((tc, QK_ROPE), F32), jnp.zeros((tc, QK_ROPE), F32)], axis=-1)

    cs = ctx
    for layer in range(depth):
        last = layer == depth - 1
        (sh_m, csh_m), (sc_m, csc_m), (gt_m, cgt_m), (sh_f, csh_f), (sc_f, csc_f), (gt_f, cgt_f) = (
            mod_rows(layer, k) for k in range(N_MOD))
        g1 = norm_g[layer, 0][None]
        g2 = norm_g[layer, 1][None]
        if layer % 2 == 0:
            e = layer // 2
            wi = w_in_e[e]
            o_kpe = Q_LORA + KV_LORA
            kpe = wi[:, o_kpe:o_kpe + QK_ROPE]
            win = jnp.concatenate([wi[:, :o_kpe], wi[:, o_kpe + QK_ROPE:], kpe, _rot_columns(kpe)], axis=1).astype(BF16)
            wq3 = w_uq[e].reshape(Q_LORA, n_heads, QK_HEAD)
            wq = jnp.concatenate([wq3, _rot_columns(wq3[..., QK_NOPE:])], axis=-1).reshape(Q_LORA, n_heads * HEAD_PAD)
            wkv3 = w_ukv[e].reshape(KV_LORA, n_heads, QK_NOPE + V_HEAD)
            wk = wkv3[..., :QK_NOPE].reshape(KV_LORA, n_heads * QK_NOPE).astype(BF16)
            wv = wkv3[..., QK_NOPE:].reshape(KV_LORA, n_heads * V_HEAD).astype(BF16)
            pre = functools.partial(_even_pre, g=g1, win=win, gqa=g_qa[e][None], gkv=g_kv[e][None],
                                    wq=wq.astype(BF16), wk=wk, wv=wv, n_heads=n_heads)
            q_l, k_l, v_l, u_l = pre(x, sc_m, sh_m, cs_tab=cs_lat)
            q_c, k_c, v_c, u_c = pre(cs, csc_m, csh_m, cs_tab=cs_ctx)
            wo = w_out_e[e].astype(BF16)
            wo_a, wo_p = wo[:n_heads * V_HEAD], wo[n_heads * V_HEAD:]
            a_l = _attention(q_l, [(k_c, v_c), (k_l, v_l)], n_heads)
            pl_l = _pool(u_l, w_pool[e].astype(BF16), pool_scale[e][None])
            x, h_l, lg_l = _mix_out([a_l, pl_l], [wo_a, wo_p], x, gt_m, g2, sc_f, sh_f, w_router[layer])
            streams = [(h_l, lg_l)]
            if not last:
                a_c = _attention(q_c, [(k_c, v_c)], n_heads)
                pl_c = _pool(u_c, w_pool[e].astype(BF16), pool_scale[e][None])
                cs, h_c, lg_c = _mix_out([a_c, pl_c], [wo_a, wo_p], cs, cgt_m, g2, csc_f, csh_f, w_router[layer])
                streams.append((h_c, lg_c))
        else:
            o = layer // 2
            wi = w_in_o[o].astype(BF16)
            lw = wi.shape[1] // 2
            p_l = _norm_matmul(x, sc_m, sh_m, g1, wi)
            p_c = _norm_matmul(cs, csc_m, csh_m, g1, wi if not last else wi[:, lw:])
            wo = w_out_o[o].astype(BF16)
            y_l = _rglru(p_l, p_c[..., p_c.shape[-1] - lw:], conv_w[o], conv_b[o], w_rg_a[o].astype(BF16), b_rg_a[o],
                         w_rg_x[o].astype(BF16), b_rg_x[o], lru_lambda[o])
            x, h_l, lg_l = _mix_out([y_l], [wo], x, gt_m, g2, sc_f, sh_f, w_router[layer])
            streams = [(h_l, lg_l)]
            if not last:
                raise NotImplementedError("context output of an RG-LRU layer is only needed when it is not the last layer")
        ys = _moe(streams, w_e1, w_e3, w_e2, layer)
        x = x + gt_f * ys[0]
        if not last:
            cs = cs + cgt_f * ys[1]
    return _final_norm(x, final_g[None])
```

```python
import functools
import math

import jax
import jax.numpy as jnp
from jax import lax
from jax.experimental import pallas as pl
from jax.experimental.pallas import tpu as pltpu

F32 = jnp.float32
BF16 = jnp.bfloat16

RMS_EPS = 1e-6
N_MOD = 6
GRID_W = 64
ROPE_BASE = 10000.0

V_HEAD = 128
QK_NOPE = 128
QK_ROPE = 64
QK_HEAD = QK_NOPE + QK_ROPE
Q_LORA = 512
KV_LORA = 512
ROPE_AXIS = QK_ROPE // 2
HEAD_PAD = 256
KEY_CHUNK = 512

POOL_WINDOWS = (2, 4, 8, 16)
POOL_PAD = 16

LRU_BLOCKS = 8
CONV_W = 4
CONV_PAD = 8
LRU_C = 8.0

N_EXPERTS = 16
EC_FACTOR = 2
MOE_ROW_TILES = 2
PAIR_CHUNK = 256

DMA_ISSUE_UNROLL = 8
SCAN_UNROLL = 8

SUBLANES = 8
LANES = 128
VMEM_LIMIT_BYTES = 56 * 1024 * 1024


def _params(n_grid):
    return pltpu.CompilerParams(dimension_semantics=("arbitrary",) * n_grid,
                                vmem_limit_bytes=VMEM_LIMIT_BYTES)


def _const_spec(shape):
    nd = len(shape)
    return pl.BlockSpec(shape, lambda *_: (0,) * nd)


def _rms_scale(xf):
    return xf * lax.rsqrt(jnp.mean(xf * xf, axis=-1, keepdims=True) + RMS_EPS)


def _dot(a, b):
    return jnp.dot(a, b, preferred_element_type=F32)


def _dot_nt(a, b):
    return lax.dot_general(a, b, (((1,), (1,)), ((), ())), preferred_element_type=F32)


def _row_tile(t, pref):
    return pref if t % pref == 0 else t


def _mods_kernel(c_ref, w_ref, b_ref, o_ref):
    c = c_ref[...]
    s = c * jax.nn.sigmoid(c)
    o_ref[0] = _dot(s.astype(BF16), w_ref[0].astype(BF16)) + b_ref[0]


def _mods(c_rows, w_ada, b_ada):
    depth, d, n = w_ada.shape
    tn = 1024
    return pl.pallas_call(
        _mods_kernel,
        grid=(depth, n // tn),
        in_specs=[_const_spec(c_rows.shape),
                  pl.BlockSpec((1, d, tn), lambda l, j: (l, 0, j)),
                  pl.BlockSpec((1, 1, tn), lambda l, j: (l, 0, j))],
        out_specs=pl.BlockSpec((1, c_rows.shape[0], tn), lambda l, j: (l, 0, j)),
        out_shape=jax.ShapeDtypeStruct((depth, c_rows.shape[0], n), F32),
        compiler_params=_params(2),
        name="adaln_mods",
    )(c_rows, w_ada, b_ada.reshape(depth, 1, n))


def _even_pre_kernel(x_ref, sc_ref, sh_ref, g_ref, win_ref, gqa_ref, gkv_ref, wq_ref, wk_ref, wv_ref, cs_ref,
                     q_ref, k_ref, v_ref, u_ref, *, n_heads, scale, pool_w):
    h = _rms_scale(x_ref[0]) * (g_ref[...] * (1.0 + sc_ref[0])) + sh_ref[0]
    p = _dot(h.astype(BF16), win_ref[...])
    cs = cs_ref[...]
    o_pool = Q_LORA + KV_LORA
    o_rope = o_pool + pool_w

    ql = _rms_scale(p[:, :Q_LORA]) * gqa_ref[...]
    q = _dot(ql.astype(BF16), wq_ref[...])
    for hd in range(n_heads):
        base = hd * HEAD_PAD
        hi = q[:, base + QK_NOPE:base + HEAD_PAD] * cs
        hi = hi + pltpu.roll(hi, QK_ROPE, axis=1)
        q_ref[0, :, base:base + QK_NOPE] = (q[:, base:base + QK_NOPE] * scale).astype(BF16)
        q_ref[0, :, base + QK_NOPE:base + HEAD_PAD] = (hi * scale).astype(BF16)

    kvl = (_rms_scale(p[:, Q_LORA:o_pool]) * gkv_ref[...]).astype(BF16)
    kn = _dot(kvl, wk_ref[...])
    v_ref[0] = _dot(kvl, wv_ref[...]).astype(BF16)
    kp = p[:, o_rope:o_rope + 2 * QK_ROPE] * cs
    kp = kp + pltpu.roll(kp, QK_ROPE, axis=1)
    lane = lax.broadcasted_iota(jnp.int32, kp.shape, 1)
    kp = jnp.where(lane < QK_ROPE, kp, 0.0).astype(BF16)
    for hd in range(n_heads):
        base = hd * HEAD_PAD
        k_ref[0, :, base:base + QK_NOPE] = kn[:, hd * QK_NOPE:(hd + 1) * QK_NOPE].astype(BF16)
        k_ref[0, :, base + QK_NOPE:base + HEAD_PAD] = kp
    u_ref[0] = p[:, o_pool:o_rope]


def _even_pre(x, sc, sh, g, win, gqa, gkv, wq, wk, wv, cs_tab, n_heads):
    b, t, d = x.shape
    tm = _row_tile(t, 256)
    pool_w = d // 2
    kern = functools.partial(_even_pre_kernel, n_heads=n_heads, scale=QK_HEAD ** -0.5 * math.log2(math.e),
                             pool_w=pool_w)
    tok = lambda w: pl.BlockSpec((1, tm, w), lambda bi, i: (bi, i, 0))
    per_b = pl.BlockSpec((1, 1, d), lambda bi, i: (bi, 0, 0))
    return pl.pallas_call(
        kern,
        grid=(b, t // tm),
        in_specs=[tok(d), per_b, per_b, _const_spec(g.shape), _const_spec(win.shape), _const_spec(gqa.shape),
                  _const_spec(gkv.shape), _const_spec(wq.shape), _const_spec(wk.shape), _const_spec(wv.shape),
                  pl.BlockSpec((tm, 2 * QK_ROPE), lambda bi, i: (i, 0))],
        out_specs=[tok(n_heads * HEAD_PAD), tok(n_heads * HEAD_PAD), tok(n_heads * V_HEAD), tok(pool_w)],
        out_shape=[jax.ShapeDtypeStruct((b, t, n_heads * HEAD_PAD), BF16),
                   jax.ShapeDtypeStruct((b, t, n_heads * HEAD_PAD), BF16),
                   jax.ShapeDtypeStruct((b, t, n_heads * V_HEAD), BF16),
                   jax.ShapeDtypeStruct((b, t, pool_w), F32)],
        compiler_params=_params(2),
        name="even_pre",
    )(x, sc, sh, g, win, gqa, gkv, wq, wk, wv, cs_tab)


def _attn_kernel(*refs, chunks):
    q_ref, o_ref = refs[0], refs[-1]
    q = q_ref[0]
    m = l = acc = None
    for i, s0, n in chunks:
        s = _dot_nt(q, refs[1 + 2 * i][0, s0:s0 + n, :])
        v = refs[2 + 2 * i][0, s0:s0 + n, :]
        mc = jnp.max(s, axis=-1, keepdims=True)
        if m is None:
            m_new = mc
            p = jnp.exp2(s - m_new)
            l = jnp.sum(p, axis=-1, keepdims=True)
            acc = _dot(p.astype(BF16), v)
        else:
            m_new = jnp.maximum(m, mc)
            alpha = jnp.exp2(m - m_new)
            p = jnp.exp2(s - m_new)
            l = alpha * l + jnp.sum(p, axis=-1, keepdims=True)
            acc = alpha * acc + _dot(p.astype(BF16), v)
        m = m_new
    o_ref[0] = (acc * (1.0 / l)).astype(BF16)


def _attention(q, kvs, n_heads):
    b, tq_all, _ = q.shape
    tq = _row_tile(tq_all, 512)
    in_specs = [pl.BlockSpec((1, tq, HEAD_PAD), lambda bi, hd, i: (bi, i, hd))]
    args = [q]
    chunks = []
    for j, (k, v) in enumerate(kvs):
        s = k.shape[1]
        tk = _row_tile(s, KEY_CHUNK)
        chunks += [(j, s0, tk) for s0 in range(0, s, tk)]
        in_specs.append(pl.BlockSpec((1, s, HEAD_PAD), lambda bi, hd, i: (bi, 0, hd)))
        in_specs.append(pl.BlockSpec((1, s, V_HEAD), lambda bi, hd, i: (bi, 0, hd)))
        args += [k, v]
    return pl.pallas_call(
        functools.partial(_attn_kernel, chunks=tuple(chunks)),
        grid=(b, n_heads, tq_all // tq),
        in_specs=in_specs,
        out_specs=pl.BlockSpec((1, tq, V_HEAD), lambda bi, hd, i: (bi, i, hd)),
        out_shape=jax.ShapeDtypeStruct((b, tq_all, n_heads * V_HEAD), BF16),
        compiler_params=_params(3),
        name="mla_attention",
    )(*args)


def _pool_kernel(u_ref, w_ref, ps_ref, o_ref, e0, e1, *, t, windows):
    gi = pl.program_id(1)
    gc = u_ref.shape[-1]
    u = u_ref[0]
    zpad = jnp.zeros((POOL_PAD, gc), F32)
    for e in (e0, e1):
        e[0:POOL_PAD, :] = zpad
        e[POOL_PAD + t:2 * POOL_PAD + t, :] = zpad
    e0[POOL_PAD:POOL_PAD + t, :] = u
    pos = lax.broadcasted_iota(jnp.int32, (t, 1), 0)
    n = t + 2 * POOL_PAD - 2 * SUBLANES
    for k, w in enumerate(windows):
        @pl.when(gi == k)
        def _(w=w):
            src, dst = e0, e1
            dst[SUBLANES:SUBLANES + n, :] = src[SUBLANES - 1:SUBLANES - 1 + n, :] + src[SUBLANES:SUBLANES + n, :]
            src, dst = dst, src
            m = 2
            while m < w:
                hm = m // 2
                dst[SUBLANES:SUBLANES + n, :] = (src[SUBLANES - hm:SUBLANES - hm + n, :]
                                                 + src[SUBLANES + hm:SUBLANES + hm + n, :])
                src, dst = dst, src
                m *= 2
            tot = src[POOL_PAD:POOL_PAD + t, :]
            cnt = (jnp.minimum(pos + (w - w // 2), t) - jnp.maximum(pos - w // 2, 0)).astype(F32)
            pooled = tot / cnt - u
            o_ref[0] = (_dot(pooled.astype(BF16), w_ref[0]) * ps_ref[...]).astype(BF16)


def _pool(u, w_pool, pool_scale):
    b, t, pw = u.shape
    ng = len(POOL_WINDOWS)
    gc = pw // ng
    return pl.pallas_call(
        functools.partial(_pool_kernel, t=t, windows=POOL_WINDOWS),
        grid=(b, ng),
        in_specs=[pl.BlockSpec((1, t, gc), lambda bi, g: (bi, 0, g)),
                  pl.BlockSpec((1, gc, gc), lambda bi, g: (g, 0, 0)),
                  pl.BlockSpec((1, gc), lambda bi, g: (0, g))],
        out_specs=pl.BlockSpec((1, t, gc), lambda bi, g: (bi, 0, g)),
        out_shape=jax.ShapeDtypeStruct((b, t, pw), BF16),
        scratch_shapes=[pltpu.VMEM((t + 2 * POOL_PAD, gc), F32), pltpu.VMEM((t + 2 * POOL_PAD, gc), F32)],
        compiler_params=_params(2),
        name="pool_mixer",
    )(u, w_pool, pool_scale)


def _mix_out_kernel(*refs, n_a):
    a_refs, w_refs = refs[:n_a], refs[n_a:2 * n_a]
    x_ref, gt_ref, g2_ref, sc_ref, sh_ref, wr_ref, xo_ref, h_ref, lg_ref = refs[2 * n_a:]
    m = functools.reduce(jnp.add, [_dot(a_refs[i][0], w_refs[i][...]) for i in range(n_a)])
    xn = x_ref[0] + gt_ref[0] * m
    xo_ref[0] = xn
    hf = _rms_scale(xn) * (g2_ref[...] * (1.0 + sc_ref[0])) + sh_ref[0]
    h_ref[0] = hf
    hb = hf.astype(BF16)
    h_lo = (hf - hb.astype(F32)).astype(BF16)
    wr = wr_ref[...]
    w_hi = wr.astype(BF16)
    w_lo = (wr - w_hi.astype(F32)).astype(BF16)
    lg_ref[0] = _dot_nt(w_hi, hb) + (_dot_nt(w_lo, hb) + _dot_nt(w_hi, h_lo))


def _mix_out(a_list, w_list, x, gt, g2, sc, sh, w_router_t):
    b, t, d = x.shape
    tm = _row_tile(t, 256)
    ne = w_router_t.shape[0]
    tok = lambda w: pl.BlockSpec((1, tm, w), lambda bi, i: (bi, i, 0))
    per_b = pl.BlockSpec((1, 1, d), lambda bi, i: (bi, 0, 0))
    in_specs = ([tok(a.shape[-1]) for a in a_list] + [_const_spec(w.shape) for w in w_list]
                + [tok(d), per_b, _const_spec(g2.shape), per_b, per_b, _const_spec(w_router_t.shape)])
    return pl.pallas_call(
        functools.partial(_mix_out_kernel, n_a=len(a_list)),
        grid=(b, t // tm),
        in_specs=in_specs,
        out_specs=[tok(d), tok(d), pl.BlockSpec((1, ne, tm), lambda bi, i: (bi, 0, i))],
        out_shape=[jax.ShapeDtypeStruct((b, t, d), F32), jax.ShapeDtypeStruct((b, t, d), F32),
                   jax.ShapeDtypeStruct((b, ne, t), F32)],
        compiler_params=_params(2),
        name="mix_out",
    )(*a_list, *w_list, x, gt, g2, sc, sh, w_router_t)


def _norm_matmul_kernel(x_ref, sc_ref, sh_ref, g_ref, w_ref, o_ref):
    h = _rms_scale(x_ref[0]) * (g_ref[...] * (1.0 + sc_ref[0])) + sh_ref[0]
    o_ref[0] = _dot(h.astype(BF16), w_ref[...])


def _norm_matmul(x, sc, sh, g, w):
    b, t, d = x.shape
    n = w.shape[1]
    tm = _row_tile(t, 256)
    tn = _row_tile(n, 2048)
    return pl.pallas_call(
        _norm_matmul_kernel,
        grid=(n // tn, b, t // tm),
        in_specs=[pl.BlockSpec((1, tm, d), lambda j, bi, i: (bi, i, 0)),
                  pl.BlockSpec((1, 1, d), lambda j, bi, i: (bi, 0, 0)),
                  pl.BlockSpec((1, 1, d), lambda j, bi, i: (bi, 0, 0)),
                  _const_spec(g.shape),
                  pl.BlockSpec((d, tn), lambda j, bi, i: (0, j))],
        out_specs=pl.BlockSpec((1, tm, tn), lambda j, bi, i: (bi, i, j)),
        out_shape=jax.ShapeDtypeStruct((b, t, n), F32),
        compiler_params=_params(3),
        name="norm_matmul",
    )(x, sc, sh, g, w)


def _rglru_kernel(gate_ref, rec_ref, recc_ref, cw_ref, cb_ref, wa_ref, ba_ref, wx_ref, bx_ref, lam_ref, o_ref,
                  ext, extc, abuf, bbuf, ybuf, *, t, tc, chunk):
    c = rec_ref.shape[-1]
    zpad = jnp.zeros((CONV_PAD, c), F32)
    for e, r, n in ((ext, rec_ref, t), (extc, recc_ref, tc)):
        e[0:CONV_PAD, :] = zpad
        e[CONV_PAD + n:2 * CONV_PAD + n, :] = zpad
        e[CONV_PAD:CONV_PAD + n, :] = r[0]
    cw = cw_ref[...]
    cb = cb_ref[...]
    row = lax.broadcasted_iota(jnp.int32, (SUBLANES, c), 0)

    def gates(e_ref, r0, n, d):
        u = cb
        for k in range(CONV_W):
            off = CONV_PAD + r0 + k - CONV_W // 2
            u = u + cw[k:k + 1] * e_ref[off:off + n, :]
        ub = u.astype(BF16)
        r = jax.nn.sigmoid(_dot(ub, wa_ref[d, 0]) + ba_ref[d:d + 1])
        i = jax.nn.sigmoid(_dot(ub, wx_ref[d, 0]) + bx_ref[d:d + 1])
        log_a = (-LRU_C * r) * jax.nn.softplus(-lam_ref[d:d + 1])
        a = jnp.exp(log_a)
        abuf[0:n, :] = a
        bbuf[0:n, :] = jnp.sqrt(1.0 - a * a) * (i * u)

    def scan(n, carry, d, emit):
        nb = n // SUBLANES

        def body(j, hc):
            jj = j if d == 0 else nb - 1 - j
            off = pl.multiple_of(jj * SUBLANES, SUBLANES)
            a = abuf[pl.ds(off, SUBLANES), :]
            bb = bbuf[pl.ds(off, SUBLANES), :]
            for s in (1, 2, 4):
                if d == 0:
                    sh, msk = s, row >= s
                else:
                    sh, msk = SUBLANES - s, row < SUBLANES - s
                a_s = jnp.where(msk, pltpu.roll(a, sh, axis=0), 1.0)
                b_s = jnp.where(msk, pltpu.roll(bb, sh, axis=0), 0.0)
                bb = a * b_s + bb
                a = a * a_s
            h = a * hc + bb
            emit(off, h)
            return h[SUBLANES - 1:SUBLANES] if d == 0 else h[0:1]

        return lax.fori_loop(0, nb, body, carry, unroll=SCAN_UNROLL)

    def run(e_ref, n_all, d, carry, emit):
        cn = min(chunk, n_all)
        starts = list(range(0, n_all, cn))
        for r0 in (starts if d == 0 else starts[::-1]):
            gates(e_ref, r0, cn, d)
            carry = scan(cn, carry, d, functools.partial(emit, r0))
        return carry

    def emit_none(r0, off, h):
        pass

    def emit_set(r0, off, h):
        ybuf[pl.ds(r0 + off, SUBLANES), :] = h

    def emit_add(r0, off, h):
        ybuf[pl.ds(r0 + off, SUBLANES), :] += h

    h0 = jnp.zeros((1, c), F32)
    run(ext, t, 0, run(extc, tc, 0, h0, emit_none), emit_set)
    run(ext, t, 1, run(extc, tc, 1, h0, emit_none), emit_add)
    cn = min(chunk, t)
    for r0 in range(0, t, cn):
        o_ref[0, r0:r0 + cn, :] = (ybuf[r0:r0 + cn, :] * jax.nn.gelu(gate_ref[0, r0:r0 + cn, :])).astype(BF16)


def _rglru(p_lat, p_ctx, conv_w, conv_b, w_a, b_a, w_x, b_x, lam):
    b, t, c2 = p_lat.shape
    cw = c2 // 2
    tc = p_ctx.shape[1]
    nb = w_a.shape[1]
    bs = cw // nb
    chunk = 512
    vec = lambda rows: pl.BlockSpec((rows, bs), lambda bi, n: (0, n))
    wspec = pl.BlockSpec((2, 1, bs, bs), lambda bi, n: (0, n, 0, 0))
    return pl.pallas_call(
        functools.partial(_rglru_kernel, t=t, tc=tc, chunk=chunk),
        grid=(b, nb),
        in_specs=[pl.BlockSpec((1, t, bs), lambda bi, n: (bi, 0, n)),
                  pl.BlockSpec((1, t, bs), lambda bi, n: (bi, 0, nb + n)),
                  pl.BlockSpec((1, tc, bs), lambda bi, n: (bi, 0, n)),
                  vec(CONV_W), vec(1), wspec, vec(2), wspec, vec(2), vec(2)],
        out_specs=pl.BlockSpec((1, t, bs), lambda bi, n: (bi, 0, n)),
        out_shape=jax.ShapeDtypeStruct((b, t, cw), BF16),
        scratch_shapes=[pltpu.VMEM((t + 2 * CONV_PAD, bs), F32), pltpu.VMEM((tc + 2 * CONV_PAD, bs), F32),
                        pltpu.VMEM((min(chunk, max(t, tc)), bs), F32), pltpu.VMEM((min(chunk, max(t, tc)), bs), F32),
                        pltpu.VMEM((t, bs), F32)],
        compiler_params=_params(2),
        name="rglru",
    )(p_lat, p_lat, p_ctx, conv_w, conv_b.reshape(1, cw), w_a, b_a, w_x, b_x, lam)


def _lane_excl_cumsum(x, tri, ones):
    r, t = x.shape
    run = jnp.zeros((r, LANES), F32)
    outs = []
    for k in range(t // LANES):
        blk = x[:, k * LANES:(k + 1) * LANES]
        outs.append(_dot(blk, tri) + run)
        run = run + _dot(blk, ones)
    return jnp.concatenate(outs, axis=1)


def _route_kernel(lg_ref, idx_ref, dst_ref, gate_ref, s_ref, cnt_ref, *, t, cap):
    bi = pl.program_id(0)
    lg = lg_ref[0]
    ne = lg.shape[0]
    ex = jnp.exp(lg - jnp.max(lg, axis=0, keepdims=True))
    aff = ex / jnp.sum(ex, axis=0, keepdims=True)

    thr = jnp.zeros((ne, 1), jnp.int32)
    for bit in range(30, -1, -1):
        cand = thr | (1 << bit)
        n_ge = jnp.sum(jnp.where(aff >= lax.bitcast_convert_type(cand, F32), 1.0, 0.0), axis=1, keepdims=True)
        thr = jnp.where(n_ge >= cap, cand, thr)
    thr_f = lax.bitcast_convert_type(thr, F32)
    gt = aff > thr_f
    eq = aff == thr_f
    need = cap - jnp.sum(jnp.where(gt, 1.0, 0.0), axis=1, keepdims=True)

    ri = lax.broadcasted_iota(jnp.int32, (LANES, LANES), 0)
    ci = lax.broadcasted_iota(jnp.int32, (LANES, LANES), 1)
    tri = jnp.where(ri < ci, 1.0, 0.0).astype(BF16)
    ones = jnp.ones((LANES, LANES), BF16)
    eq_rank = _lane_excl_cumsum(jnp.where(eq, 1.0, 0.0).astype(BF16), tri, ones)
    sel = gt | (eq & (eq_rank < need))
    sel_f = jnp.where(sel, 1.0, 0.0)
    sel_b = sel_f.astype(BF16)
    pos = _lane_excl_cumsum(sel_b, tri, ones)
    posm = jnp.where(sel, pos, -1.0).astype(jnp.int32)
    cnt = jnp.sum(sel_f, axis=0, keepdims=True)
    re = lax.broadcasted_iota(jnp.int32, (ne, ne), 0)
    ce = lax.broadcasted_iota(jnp.int32, (ne, ne), 1)
    earlier = _dot(jnp.where(ce < re, 1.0, 0.0).astype(BF16), sel_b)
    s_tok = _lane_excl_cumsum(jnp.broadcast_to(cnt, (ne, t)).astype(BF16), tri, ones)[0:1]
    slot = s_tok + earlier

    tpos = lax.broadcasted_iota(jnp.int32, (1, t), 1)
    t_hi = (tpos >> 6).astype(F32)
    t_lo = (tpos & 63).astype(F32)
    d_hi = jnp.floor(slot * (1.0 / LANES))
    d_lo = slot - d_hi * LANES
    g_hi = aff.astype(BF16).astype(F32)
    g_mid = (aff - g_hi).astype(BF16).astype(F32)
    g_lo = (aff - g_hi) - g_mid
    zrows = jnp.zeros((2 * SUBLANES - 7, t), F32)
    p_iota = lax.broadcasted_iota(jnp.int32, (cap, t), 0)
    for e in range(ne):
        onehot = jnp.where(posm[e:e + 1, :] == p_iota, 1.0, 0.0).astype(BF16)
        vals = jnp.concatenate([t_hi, t_lo, d_hi[e:e + 1], d_lo[e:e + 1], g_hi[e:e + 1], g_mid[e:e + 1],
                                g_lo[e:e + 1], zrows], axis=0).astype(BF16)
        out = _dot_nt(vals, onehot)
        idx_ref[0, e:e + 1, :] = (out[0:1] * 64.0 + out[1:2]).astype(jnp.int32) + bi * t
        dst_ref[0, e:e + 1, :] = (out[2:3] * float(LANES) + out[3:4]).astype(jnp.int32) + bi * (EC_FACTOR * t)
        gate_ref[0, e:e + 1, :] = (out[4:5] + out[5:6]) + out[6:7]
    s_ref[0] = s_tok.astype(jnp.int32)
    cnt_ref[0] = cnt.astype(jnp.int32)


def _route(lg_t):
    b, ne, t = lg_t.shape
    cap = EC_FACTOR * t // ne
    lst = lambda dt: jax.ShapeDtypeStruct((b, ne, cap), dt)
    tokrow = lambda dt: jax.ShapeDtypeStruct((b, 1, t), dt)
    per_b = lambda s: pl.BlockSpec((1,) + s, lambda bi: (bi, 0, 0))
    return pl.pallas_call(
        functools.partial(_route_kernel, t=t, cap=cap),
        grid=(b,),
        in_specs=[per_b((ne, t))],
        out_specs=[per_b((ne, cap)), per_b((ne, cap)), per_b((ne, cap)), per_b((1, t)), per_b((1, t))],
        out_shape=[lst(jnp.int32), lst(jnp.int32), lst(F32), tokrow(jnp.int32), tokrow(jnp.int32)],
        compiler_params=_params(1),
        name="moe_route",
    )(lg_t)


def _moe_ffn_kernel(*refs, n_rt, part_rows):
    n_src = len(part_rows)
    idx_ref, idxn_ref, dst_ref, g_ref = refs[:4]
    h_refs = refs[4:4 + n_src]
    w1_ref, w3_ref, w2_ref = refs[4 + n_src:7 + n_src]
    z_refs = refs[7 + n_src:7 + 2 * n_src]
    stage, xb, acc, ybuf, sem = refs[7 + 2 * n_src:]
    tr = sum(part_rows)
    r = pl.program_id(1)
    f = pl.program_id(2)
    tile = pl.program_id(0) * n_rt + r
    n_tiles = pl.num_programs(0) * n_rt
    starts = [sum(part_rows[:i]) for i in range(n_src)]

    def gather(ix_ref):
        for lo, n, h_ref in zip(starts, part_rows, h_refs):
            def issue(p, c, h_ref=h_ref):
                pltpu.make_async_copy(h_ref.at[pl.ds(ix_ref[0, 0, p], 1)], stage.at[pl.ds(p, 1)], sem.at[0]).start()
                return c
            lax.fori_loop(lo, lo + n, issue, 0, unroll=DMA_ISSUE_UNROLL)

    def scatter():
        for lo, n, z_ref in zip(starts, part_rows, z_refs):
            def issue(p, c, z_ref=z_ref):
                pltpu.make_async_copy(ybuf.at[pl.ds(p, 1)], z_ref.at[pl.ds(dst_ref[0, 0, p], 1)], sem.at[1]).start()
                return c
            lax.fori_loop(lo, lo + n, issue, 0, unroll=DMA_ISSUE_UNROLL)

    gather_done = pltpu.make_async_copy(h_refs[0].at[pl.ds(0, tr)], stage, sem.at[0])
    scatter_done = pltpu.make_async_copy(ybuf, z_refs[0].at[pl.ds(0, tr)], sem.at[1])

    @pl.when(f == 0)
    def _():
        @pl.when(tile == 0)
        def _():
            gather(idx_ref)
        gather_done.wait()
        xb[...] = stage[...].astype(BF16)
        acc[...] = jnp.zeros_like(acc)

        @pl.when(tile + 1 < n_tiles)
        def _():
            gather(idxn_ref)

    x = xb[...]
    a1 = _dot(x, w1_ref[0, 0].astype(BF16))
    a3 = _dot(x, w3_ref[0, 0].astype(BF16))
    hmid = (a1 * jax.nn.sigmoid(a1) * a3).astype(BF16)
    acc[...] += _dot(hmid, w2_ref[0, 0].astype(BF16))

    @pl.when(f == pl.num_programs(2) - 1)
    def _():
        @pl.when(tile > 0)
        def _():
            scatter_done.wait()
        ybuf[...] = acc[...] * g_ref[0]
        scatter()

        @pl.when(tile == n_tiles - 1)
        def _():
            scatter_done.wait()


def _moe_ffn(idx, dst, gates, hs, z_rows, w1, w3, w2, layer, part_rows):
    e, n_rt, tr = idx.shape
    d = hs[0].shape[-1]
    ff = w1.shape[-1]
    tf = _row_tile(ff, 256)
    n_src = len(hs)
    tiles = lambda a: a.reshape(e * n_rt, 1, tr)
    smem = lambda imap: pl.BlockSpec((1, 1, tr), imap, memory_space=pltpu.SMEM)
    cur = lambda ei, ri, f: (ei * n_rt + ri, 0, 0)
    nxt = lambda ei, ri, f: (jnp.minimum(ei * n_rt + ri + 1, e * n_rt - 1), 0, 0)
    any_spec = pl.BlockSpec(memory_space=pl.ANY)
    return pl.pallas_call(
        functools.partial(_moe_ffn_kernel, n_rt=n_rt, part_rows=tuple(part_rows)),
        grid=(e, n_rt, ff // tf),
        in_specs=[smem(cur), smem(nxt), smem(cur),
                  pl.BlockSpec((1, tr, 1), lambda ei, ri, f: (ei, ri, 0))]
                 + [any_spec] * n_src
                 + [pl.BlockSpec((1, 1, d, tf), lambda ei, ri, f: (layer, ei, 0, f)),
                    pl.BlockSpec((1, 1, d, tf), lambda ei, ri, f: (layer, ei, 0, f)),
                    pl.BlockSpec((1, 1, tf, d), lambda ei, ri, f: (layer, ei, f, 0))],
        out_specs=[any_spec] * n_src,
        out_shape=[jax.ShapeDtypeStruct((n, d), F32) for n in z_rows],
        scratch_shapes=[pltpu.VMEM((tr, d), F32), pltpu.VMEM((tr, d), BF16), pltpu.VMEM((tr, d), F32),
                        pltpu.VMEM((tr, d), F32), pltpu.SemaphoreType.DMA((2,))],
        compiler_params=_params(3),
        name="moe_ffn",
    )(tiles(idx), tiles(idx), tiles(dst), gates.reshape(e, n_rt * tr, 1), *hs, w1, w3, w2)


def _combine_kernel(clo_ref, chi_ref, x_ref, gt_ref, s_ref, cnt_ref, z_ref, fg_ref, o_ref, zbuf, acc, sem, *,
                    pairs_per_sample, final):
    bi = pl.program_id(0)
    i = pl.program_id(1)
    lo = clo_ref[bi * pl.num_programs(1) + i]
    hi = chi_ref[bi * pl.num_programs(1) + i]
    base = bi * pairs_per_sample

    def fetch(k, slot):
        return pltpu.make_async_copy(z_ref.at[pl.ds(base + k * PAIR_CHUNK, PAIR_CHUNK)], zbuf.at[slot], sem.at[slot])

    @pl.when(lo < hi)
    def _():
        fetch(lo, 0).start()

    acc[...] = jnp.zeros_like(acc)
    s0 = s_ref[0]
    s1 = s0 + cnt_ref[0]

    def body(k, c):
        slot = lax.rem(k - lo, 2)

        @pl.when(k + 1 < hi)
        def _():
            fetch(k + 1, 1 - slot).start()

        fetch(k, slot).wait()
        sl = k * PAIR_CHUNK + lax.broadcasted_iota(jnp.int32, (1, PAIR_CHUNK), 1)
        own = jnp.where((sl >= s0) & (sl < s1), 1.0, 0.0).astype(BF16)
        acc[...] += _dot(own, zbuf[slot].astype(BF16))
        return c

    lax.fori_loop(lo, hi, body, 0)
    xn = x_ref[0] + gt_ref[0] * acc[...]
    o_ref[0] = _rms_scale(xn) * fg_ref[...] if final else xn


def _combine(x, gt, s_tok, cnt, z, final_g, final):
    b, t, d = x.shape
    tm = _row_tile(t, 256)
    nt = t // tm
    pairs = z.shape[0] // b
    s2 = s_tok.reshape(b, t)
    s_end = s2 + cnt.reshape(b, t)
    clo = (s2[:, ::tm] // PAIR_CHUNK).reshape(-1)
    chi = ((s_end[:, tm - 1::tm] + PAIR_CHUNK - 1) // PAIR_CHUNK).reshape(-1)
    tok = lambda w: pl.BlockSpec((1, tm, w), lambda bi, i, *_: (bi, i, 0))
    return pl.pallas_call(
        functools.partial(_combine_kernel, pairs_per_sample=pairs, final=final),
        grid_spec=pltpu.PrefetchScalarGridSpec(
            num_scalar_prefetch=2,
            grid=(b, nt),
            in_specs=[tok(d), pl.BlockSpec((1, 1, d), lambda bi, i, *_: (bi, 0, 0)), tok(1), tok(1),
                      pl.BlockSpec(memory_space=pl.ANY), pl.BlockSpec(final_g.shape, lambda bi, i, *_: (0, 0))],
            out_specs=tok(d),
            scratch_shapes=[pltpu.VMEM((2, PAIR_CHUNK, d), F32), pltpu.VMEM((tm, d), F32),
                            pltpu.SemaphoreType.DMA((2,))]),
        out_shape=jax.ShapeDtypeStruct((b, t, d), F32),
        compiler_params=_params(2),
        name="moe_combine",
    )(clo, chi, x, gt, s_tok.reshape(b, t, 1), cnt.reshape(b, t, 1), z, final_g)


def _rope_tables(t):
    rows = t // GRID_W
    row = jnp.repeat(jnp.arange(rows, dtype=F32), GRID_W)
    col = jnp.tile(jnp.arange(GRID_W, dtype=F32), rows)
    inv = ROPE_BASE ** (-jnp.arange(0, ROPE_AXIS, 2, dtype=F32) / ROPE_AXIS)
    ar = row[:, None] * inv
    ac = col[:, None] * inv
    cos = jnp.concatenate([jnp.cos(ar), jnp.cos(ar), jnp.cos(ac), jnp.cos(ac)], axis=-1)
    sin = jnp.concatenate([jnp.sin(ar), jnp.sin(ar), jnp.sin(ac), jnp.sin(ac)], axis=-1)
    return jnp.concatenate([cos, sin], axis=-1)


def _rot_columns(w):
    half = ROPE_AXIS // 2
    return jnp.concatenate([-w[..., half:ROPE_AXIS], w[..., :half],
                            -w[..., ROPE_AXIS + half:], w[..., ROPE_AXIS:ROPE_AXIS + half]], axis=-1)


def _moe(streams, w1, w3, w2, layer):
    ne = w1.shape[1]
    n_rt = MOE_ROW_TILES
    idxs, dsts, gates, hs, z_rows, toks = [], [], [], [], [], []
    for h, lg_t in streams:
        b, t, d = h.shape
        idx, dst, gate, s_tok, cnt = _route(lg_t)
        by_tile = lambda a: jnp.swapaxes(a, 0, 1).reshape(ne, n_rt, -1)
        idxs.append(by_tile(idx))
        dsts.append(by_tile(dst))
        gates.append(by_tile(gate))
        hs.append(h.reshape(b * t, d))
        z_rows.append(b * EC_FACTOR * t)
        toks.append((s_tok, cnt))
    zs = _moe_ffn(jnp.concatenate(idxs, axis=2), jnp.concatenate(dsts, axis=2), jnp.concatenate(gates, axis=2),
                  hs, z_rows, w1, w3, w2, layer, [a.shape[2] for a in idxs])
    return [(z,) + tk for z, tk in zip(zs, toks)]


def kernel(x, c, ctx, c_ctx, w_ada, b_ada, norm_g, final_g, w_in_e, g_qa, w_uq, g_kv, w_ukv, w_pool, pool_scale,
           w_out_e, w_in_o, conv_w, conv_b, w_rg_a, b_rg_a, w_rg_x, b_rg_x, lru_lambda, w_out_o, w_router, w_e1,
           w_e3, w_e2):
    b, t, d = x.shape
    tc = ctx.shape[1]
    depth = w_ada.shape[0]
    n_heads = d // (2 * V_HEAD)
    pool_w = d // 2

    c_rows = jnp.concatenate([c, c_ctx[None], jnp.zeros((SUBLANES - b - 1, d), F32)], axis=0)
    mods = _mods(c_rows, w_ada, b_ada)

    def mod_rows(layer, k):
        m = mods[layer, :, k * d:(k + 1) * d]
        return m[:b, None, :], jnp.broadcast_to(m[b][None, None, :], (b, 1, d))

    cs_lat = _rope_tables(t)
    cs_ctx = jnp.concatenate([jnp.ones((tc, QK_ROPE), F32), jnp.zeros((tc, QK_ROPE), F32)], axis=-1)

    cs = ctx
    for layer in range(depth):
        last = layer == depth - 1
        (sh_m, csh_m), (sc_m, csc_m), (gt_m, cgt_m), (sh_f, csh_f), (sc_f, csc_f), (gt_f, cgt_f) = (
            mod_rows(layer, k) for k in range(N_MOD))
        g1 = norm_g[layer, 0][None]
        g2 = norm_g[layer, 1][None]
        if layer % 2 == 0:
            e = layer // 2
            wi = w_in_e[e]
            o_kpe = Q_LORA + KV_LORA
            kpe = wi[:, o_kpe:o_kpe + QK_ROPE]
            win = jnp.concatenate([wi[:, :o_kpe], wi[:, o_kpe + QK_ROPE:], kpe, _rot_columns(kpe)], axis=1).astype(BF16)
            wq3 = w_uq[e].reshape(Q_LORA, n_heads, QK_HEAD)
            wq = jnp.concatenate([wq3, _rot_columns(wq3[..., QK_NOPE:])], axis=-1).reshape(Q_LORA, n_heads * HEAD_PAD)
            wkv3 = w_ukv[e].reshape(KV_LORA, n_heads, QK_NOPE + V_HEAD)
            wk = wkv3[..., :QK_NOPE].reshape(KV_LORA, n_heads * QK_NOPE).astype(BF16)
            wv = wkv3[..., QK_NOPE:].reshape(KV_LORA, n_heads * V_HEAD).astype(BF16)
            pre = functools.partial(_even_pre, g=g1, win=win, gqa=g_qa[e][None], gkv=g_kv[e][None],
                                    wq=wq.astype(BF16), wk=wk, wv=wv, n_heads=n_heads)
            q_l, k_l, v_l, u_l = pre(x, sc_m, sh_m, cs_tab=cs_lat)
            q_c, k_c, v_c, u_c = pre(cs, csc_m, csh_m, cs_tab=cs_ctx)
            wo = w_out_e[e].astype(BF16)
            wo_a, wo_p = wo[:n_heads * V_HEAD], wo[n_heads * V_HEAD:]
            a_l = _attention(q_l, [(k_c, v_c), (k_l, v_l)], n_heads)
            pl_l = _pool(u_l, w_pool[e].astype(BF16), pool_scale[e][None])
            x, h_l, lg_l = _mix_out([a_l, pl_l], [wo_a, wo_p], x, gt_m, g2, sc_f, sh_f, w_router[layer].T)
            streams = [(h_l, lg_l)]
            if not last:
                a_c = _attention(q_c, [(k_c, v_c)], n_heads)
                pl_c = _pool(u_c, w_pool[e].astype(BF16), pool_scale[e][None])
                cs, h_c, lg_c = _mix_out([a_c, pl_c], [wo_a, wo_p], cs, cgt_m, g2, csc_f, csh_f, w_router[layer].T)
                streams.append((h_c, lg_c))
        else:
            o = layer // 2
            wi = w_in_o[o].astype(BF16)
            lw = wi.shape[1] // 2
            p_l = _norm_matmul(x, sc_m, sh_m, g1, wi)
            p_c = _norm_matmul(cs, csc_m, csh_m, g1, wi if not last else wi[:, lw:])
            wo = w_out_o[o].astype(BF16)
            y_l = _rglru(p_l, p_c[..., p_c.shape[-1] - lw:], conv_w[o], conv_b[o], w_rg_a[o].astype(BF16), b_rg_a[o],
                         w_rg_x[o].astype(BF16), b_rg_x[o], lru_lambda[o])
            x, h_l, lg_l = _mix_out([y_l], [wo], x, gt_m, g2, sc_f, sh_f, w_router[layer].T)
            streams = [(h_l, lg_l)]
            if not last:
                raise NotImplementedError("context output of an RG-LRU layer is only needed when it is not the last layer")
        pairs = _moe(streams, w_e1, w_e3, w_e2, layer)
        z_l, s_l, n_l = pairs[0]
        x = _combine(x, gt_f, s_l, n_l, z_l, final_g[None], final=last)
        if not last:
            z_c, s_c, n_c = pairs[1]
            cs = _combine(cs, cgt_f, s_c, n_c, z_c, final_g[None], final=False)
    return x
```

```python
import functools
import math

import jax
import jax.numpy as jnp
from jax import lax
from jax.experimental import pallas as pl
from jax.experimental.pallas import tpu as pltpu

F32 = jnp.float32
BF16 = jnp.bfloat16

RMS_EPS = 1e-6
N_MOD = 6
GRID_W = 64
ROPE_BASE = 10000.0

V_HEAD = 128
QK_NOPE = 128
QK_ROPE = 64
QK_HEAD = QK_NOPE + QK_ROPE
Q_LORA = 512
KV_LORA = 512
ROPE_AXIS = QK_ROPE // 2
HEAD_PAD = 256
KEY_CHUNK = 256

POOL_WINDOWS = (2, 4, 8, 16)
POOL_PAD = 16

LRU_BLOCKS = 8
CONV_W = 4
CONV_PAD = 8
LRU_C = 8.0

N_EXPERTS = 16
EC_FACTOR = 2
MOE_ROW_TILES = 2
MOE_TOKEN_TILE = 256
MOE_WINDOW = 64

DMA_ISSUE_UNROLL = 8
SCAN_UNROLL = 8

SUBLANES = 8
LANES = 128
VMEM_LIMIT_BYTES = 56 * 1024 * 1024


def _params(n_grid):
    return pltpu.CompilerParams(dimension_semantics=("arbitrary",) * n_grid,
                                vmem_limit_bytes=VMEM_LIMIT_BYTES)


def _const_spec(shape):
    nd = len(shape)
    return pl.BlockSpec(shape, lambda *_: (0,) * nd)


def _rms_scale(xf):
    return xf * lax.rsqrt(jnp.mean(xf * xf, axis=-1, keepdims=True) + RMS_EPS)


def _dot(a, b):
    return jnp.dot(a, b, preferred_element_type=F32)


def _dot_nt(a, b):
    return lax.dot_general(a, b, (((1,), (1,)), ((), ())), preferred_element_type=F32)


def _row_tile(t, pref):
    return pref if t % pref == 0 else t


def _mods_kernel(c_ref, w_ref, b_ref, o_ref):
    c = c_ref[...]
    s = c * jax.nn.sigmoid(c)
    o_ref[0] = _dot(s.astype(BF16), w_ref[0].astype(BF16)) + b_ref[0]


def _mods(c_rows, w_ada, b_ada):
    depth, d, n = w_ada.shape
    tn = 1024
    return pl.pallas_call(
        _mods_kernel,
        grid=(depth, n // tn),
        in_specs=[_const_spec(c_rows.shape),
                  pl.BlockSpec((1, d, tn), lambda l, j: (l, 0, j)),
                  pl.BlockSpec((1, 1, tn), lambda l, j: (l, 0, j))],
        out_specs=pl.BlockSpec((1, c_rows.shape[0], tn), lambda l, j: (l, 0, j)),
        out_shape=jax.ShapeDtypeStruct((depth, c_rows.shape[0], n), F32),
        compiler_params=_params(2),
        name="adaln_mods",
    )(c_rows, w_ada, b_ada.reshape(depth, 1, n))


def _even_pre_kernel(x_ref, sc_ref, sh_ref, g_ref, win_ref, gqa_ref, gkv_ref, wq_ref, wk_ref, wv_ref, cs_ref,
                     q_ref, k_ref, v_ref, u_ref, *, n_heads, scale, pool_w):
    h = _rms_scale(x_ref[0]) * (g_ref[...] * (1.0 + sc_ref[0])) + sh_ref[0]
    p = _dot(h.astype(BF16), win_ref[...])
    cs = cs_ref[...]
    o_pool = Q_LORA + KV_LORA
    o_rope = o_pool + pool_w

    ql = _rms_scale(p[:, :Q_LORA]) * gqa_ref[...]
    q = _dot(ql.astype(BF16), wq_ref[...])
    for hd in range(n_heads):
        base = hd * HEAD_PAD
        hi = q[:, base + QK_NOPE:base + HEAD_PAD] * cs
        hi = hi + pltpu.roll(hi, QK_ROPE, axis=1)
        q_ref[0, :, base:base + QK_NOPE] = (q[:, base:base + QK_NOPE] * scale).astype(BF16)
        q_ref[0, :, base + QK_NOPE:base + HEAD_PAD] = (hi * scale).astype(BF16)

    kvl = (_rms_scale(p[:, Q_LORA:o_pool]) * gkv_ref[...]).astype(BF16)
    kn = _dot(kvl, wk_ref[...])
    v_ref[0] = _dot(kvl, wv_ref[...]).astype(BF16)
    kp = p[:, o_rope:o_rope + 2 * QK_ROPE] * cs
    kp = kp + pltpu.roll(kp, QK_ROPE, axis=1)
    lane = lax.broadcasted_iota(jnp.int32, kp.shape, 1)
    kp = jnp.where(lane < QK_ROPE, kp, 0.0).astype(BF16)
    for hd in range(n_heads):
        base = hd * HEAD_PAD
        k_ref[0, :, base:base + QK_NOPE] = kn[:, hd * QK_NOPE:(hd + 1) * QK_NOPE].astype(BF16)
        k_ref[0, :, base + QK_NOPE:base + HEAD_PAD] = kp
    u_ref[0] = p[:, o_pool:o_rope]


def _even_pre(x, sc, sh, g, win, gqa, gkv, wq, wk, wv, cs_tab, n_heads):
    b, t, d = x.shape
    tm = _row_tile(t, 256)
    pool_w = d // 2
    kern = functools.partial(_even_pre_kernel, n_heads=n_heads, scale=QK_HEAD ** -0.5 * math.log2(math.e),
                             pool_w=pool_w)
    tok = lambda w: pl.BlockSpec((1, tm, w), lambda bi, i: (bi, i, 0))
    per_b = pl.BlockSpec((1, 1, d), lambda bi, i: (bi, 0, 0))
    return pl.pallas_call(
        kern,
        grid=(b, t // tm),
        in_specs=[tok(d), per_b, per_b, _const_spec(g.shape), _const_spec(win.shape), _const_spec(gqa.shape),
                  _const_spec(gkv.shape), _const_spec(wq.shape), _const_spec(wk.shape), _const_spec(wv.shape),
                  pl.BlockSpec((tm, 2 * QK_ROPE), lambda bi, i: (i, 0))],
        out_specs=[tok(n_heads * HEAD_PAD), tok(n_heads * HEAD_PAD), tok(n_heads * V_HEAD), tok(pool_w)],
        out_shape=[jax.ShapeDtypeStruct((b, t, n_heads * HEAD_PAD), BF16),
                   jax.ShapeDtypeStruct((b, t, n_heads * HEAD_PAD), BF16),
                   jax.ShapeDtypeStruct((b, t, n_heads * V_HEAD), BF16),
                   jax.ShapeDtypeStruct((b, t, pool_w), F32)],
        compiler_params=_params(2),
        name="even_pre",
    )(x, sc, sh, g, win, gqa, gkv, wq, wk, wv, cs_tab)


def _attn_kernel(*refs, chunks):
    q_ref, o_ref = refs[0], refs[-1]
    q = q_ref[0]
    m = l = acc = None
    for i, s0, n in chunks:
        s = _dot_nt(q, refs[1 + 2 * i][0, s0:s0 + n, :])
        v = refs[2 + 2 * i][0, s0:s0 + n, :]
        mc = jnp.max(s, axis=-1, keepdims=True)
        if m is None:
            m_new = mc
            p = jnp.exp2(s - m_new)
            l = jnp.sum(p, axis=-1, keepdims=True)
            acc = _dot(p.astype(BF16), v)
        else:
            m_new = jnp.maximum(m, mc)
            alpha = jnp.exp2(m - m_new)
            p = jnp.exp2(s - m_new)
            l = alpha * l + jnp.sum(p, axis=-1, keepdims=True)
            acc = alpha * acc + _dot(p.astype(BF16), v)
        m = m_new
    o_ref[0] = (acc * (1.0 / l)).astype(BF16)


def _attention(q, kvs, n_heads):
    b, tq_all, _ = q.shape
    tq = _row_tile(tq_all, 2048)
    in_specs = [pl.BlockSpec((1, tq, HEAD_PAD), lambda bi, hd, i: (bi, i, hd))]
    args = [q]
    chunks = []
    for j, (k, v) in enumerate(kvs):
        s = k.shape[1]
        tk = _row_tile(s, KEY_CHUNK)
        chunks += [(j, s0, tk) for s0 in range(0, s, tk)]
        in_specs.append(pl.BlockSpec((1, s, HEAD_PAD), lambda bi, hd, i: (bi, 0, hd)))
        in_specs.append(pl.BlockSpec((1, s, V_HEAD), lambda bi, hd, i: (bi, 0, hd)))
        args += [k, v]
    return pl.pallas_call(
        functools.partial(_attn_kernel, chunks=tuple(chunks)),
        grid=(b, n_heads, tq_all // tq),
        in_specs=in_specs,
        out_specs=pl.BlockSpec((1, tq, V_HEAD), lambda bi, hd, i: (bi, i, hd)),
        out_shape=jax.ShapeDtypeStruct((b, tq_all, n_heads * V_HEAD), BF16),
        compiler_params=_params(3),
        name="mla_attention",
    )(*args)


def _pool_kernel(u_ref, w_ref, ps_ref, o_ref, e0, e1, *, t, windows):
    gi = pl.program_id(1)
    gc = u_ref.shape[-1]
    u = u_ref[0]
    zpad = jnp.zeros((POOL_PAD, gc), F32)
    for e in (e0, e1):
        e[0:POOL_PAD, :] = zpad
        e[POOL_PAD + t:2 * POOL_PAD + t, :] = zpad
    e0[POOL_PAD:POOL_PAD + t, :] = u
    pos = lax.broadcasted_iota(jnp.int32, (t, 1), 0)
    n = t + 2 * POOL_PAD - 2 * SUBLANES
    for k, w in enumerate(windows):
        @pl.when(gi == k)
        def _(w=w):
            src, dst = e0, e1
            dst[SUBLANES:SUBLANES + n, :] = src[SUBLANES - 1:SUBLANES - 1 + n, :] + src[SUBLANES:SUBLANES + n, :]
            src, dst = dst, src
            m = 2
            while m < w:
                hm = m // 2
                dst[SUBLANES:SUBLANES + n, :] = (src[SUBLANES - hm:SUBLANES - hm + n, :]
                                                 + src[SUBLANES + hm:SUBLANES + hm + n, :])
                src, dst = dst, src
                m *= 2
            tot = src[POOL_PAD:POOL_PAD + t, :]
            cnt = (jnp.minimum(pos + (w - w // 2), t) - jnp.maximum(pos - w // 2, 0)).astype(F32)
            pooled = tot / cnt - u
            o_ref[0] = (_dot(pooled.astype(BF16), w_ref[0]) * ps_ref[...]).astype(BF16)


def _pool(u, w_pool, pool_scale):
    b, t, pw = u.shape
    ng = len(POOL_WINDOWS)
    gc = pw // ng
    return pl.pallas_call(
        functools.partial(_pool_kernel, t=t, windows=POOL_WINDOWS),
        grid=(b, ng),
        in_specs=[pl.BlockSpec((1, t, gc), lambda bi, g: (bi, 0, g)),
                  pl.BlockSpec((1, gc, gc), lambda bi, g: (g, 0, 0)),
                  pl.BlockSpec((1, gc), lambda bi, g: (0, g))],
        out_specs=pl.BlockSpec((1, t, gc), lambda bi, g: (bi, 0, g)),
        out_shape=jax.ShapeDtypeStruct((b, t, pw), BF16),
        scratch_shapes=[pltpu.VMEM((t + 2 * POOL_PAD, gc), F32), pltpu.VMEM((t + 2 * POOL_PAD, gc), F32)],
        compiler_params=_params(2),
        name="pool_mixer",
    )(u, w_pool, pool_scale)


def _mix_out_kernel(*refs, n_a):
    a_refs, w_refs = refs[:n_a], refs[n_a:2 * n_a]
    x_ref, gt_ref, g2_ref, sc_ref, sh_ref, wr_ref, xo_ref, h_ref, lg_ref = refs[2 * n_a:]
    m = functools.reduce(jnp.add, [_dot(a_refs[i][0], w_refs[i][...]) for i in range(n_a)])
    xn = x_ref[0] + gt_ref[0] * m
    xo_ref[0] = xn
    hf = _rms_scale(xn) * (g2_ref[...] * (1.0 + sc_ref[0])) + sh_ref[0]
    h_ref[0] = hf
    hb = hf.astype(BF16)
    h_lo = (hf - hb.astype(F32)).astype(BF16)
    wr = wr_ref[...]
    w_hi = wr.astype(BF16)
    w_lo = (wr - w_hi.astype(F32)).astype(BF16)
    lg_ref[0] = _dot_nt(w_hi, hb) + (_dot_nt(w_lo, hb) + _dot_nt(w_hi, h_lo))


def _mix_out(a_list, w_list, x, gt, g2, sc, sh, w_router_t):
    b, t, d = x.shape
    tm = _row_tile(t, 256)
    ne = w_router_t.shape[0]
    tok = lambda w: pl.BlockSpec((1, tm, w), lambda bi, i: (bi, i, 0))
    per_b = pl.BlockSpec((1, 1, d), lambda bi, i: (bi, 0, 0))
    in_specs = ([tok(a.shape[-1]) for a in a_list] + [_const_spec(w.shape) for w in w_list]
                + [tok(d), per_b, _const_spec(g2.shape), per_b, per_b, _const_spec(w_router_t.shape)])
    return pl.pallas_call(
        functools.partial(_mix_out_kernel, n_a=len(a_list)),
        grid=(b, t // tm),
        in_specs=in_specs,
        out_specs=[tok(d), tok(d), pl.BlockSpec((1, ne, tm), lambda bi, i: (bi, 0, i))],
        out_shape=[jax.ShapeDtypeStruct((b, t, d), F32), jax.ShapeDtypeStruct((b, t, d), F32),
                   jax.ShapeDtypeStruct((b, ne, t), F32)],
        compiler_params=_params(2),
        name="mix_out",
    )(*a_list, *w_list, x, gt, g2, sc, sh, w_router_t)


def _norm_matmul_kernel(x_ref, sc_ref, sh_ref, g_ref, w_ref, o_ref):
    h = _rms_scale(x_ref[0]) * (g_ref[...] * (1.0 + sc_ref[0])) + sh_ref[0]
    o_ref[0] = _dot(h.astype(BF16), w_ref[...])


def _norm_matmul(x, sc, sh, g, w):
    b, t, d = x.shape
    n = w.shape[1]
    tm = _row_tile(t, 256)
    tn = _row_tile(n, 2048)
    return pl.pallas_call(
        _norm_matmul_kernel,
        grid=(n // tn, b, t // tm),
        in_specs=[pl.BlockSpec((1, tm, d), lambda j, bi, i: (bi, i, 0)),
                  pl.BlockSpec((1, 1, d), lambda j, bi, i: (bi, 0, 0)),
                  pl.BlockSpec((1, 1, d), lambda j, bi, i: (bi, 0, 0)),
                  _const_spec(g.shape),
                  pl.BlockSpec((d, tn), lambda j, bi, i: (0, j))],
        out_specs=pl.BlockSpec((1, tm, tn), lambda j, bi, i: (bi, i, j)),
        out_shape=jax.ShapeDtypeStruct((b, t, n), F32),
        compiler_params=_params(3),
        name="norm_matmul",
    )(x, sc, sh, g, w)


def _rglru_kernel(gate_ref, rec_ref, recc_ref, cw_ref, cb_ref, wa_ref, ba_ref, wx_ref, bx_ref, lam_ref, o_ref,
                  ext, extc, abuf, bbuf, ybuf, *, t, tc, chunk):
    c = rec_ref.shape[-1]
    zpad = jnp.zeros((CONV_PAD, c), F32)
    for e, r, n in ((ext, rec_ref, t), (extc, recc_ref, tc)):
        e[0:CONV_PAD, :] = zpad
        e[CONV_PAD + n:2 * CONV_PAD + n, :] = zpad
        e[CONV_PAD:CONV_PAD + n, :] = r[0]
    cw = cw_ref[...]
    cb = cb_ref[...]
    row = lax.broadcasted_iota(jnp.int32, (SUBLANES, c), 0)

    def gates(e_ref, r0, n, d):
        u = cb
        for k in range(CONV_W):
            off = CONV_PAD + r0 + k - CONV_W // 2
            u = u + cw[k:k + 1] * e_ref[off:off + n, :]
        ub = u.astype(BF16)
        r = jax.nn.sigmoid(_dot(ub, wa_ref[d, 0]) + ba_ref[d:d + 1])
        i = jax.nn.sigmoid(_dot(ub, wx_ref[d, 0]) + bx_ref[d:d + 1])
        log_a = (-LRU_C * r) * jax.nn.softplus(-lam_ref[d:d + 1])
        a = jnp.exp(log_a)
        abuf[0:n, :] = a
        bbuf[0:n, :] = jnp.sqrt(1.0 - a * a) * (i * u)

    def scan(n, carry, d, emit):
        nb = n // SUBLANES

        def body(j, hc):
            jj = j if d == 0 else nb - 1 - j
            off = pl.multiple_of(jj * SUBLANES, SUBLANES)
            a = abuf[pl.ds(off, SUBLANES), :]
            bb = bbuf[pl.ds(off, SUBLANES), :]
            for s in (1, 2, 4):
                if d == 0:
                    sh, msk = s, row >= s
                else:
                    sh, msk = SUBLANES - s, row < SUBLANES - s
                a_s = jnp.where(msk, pltpu.roll(a, sh, axis=0), 1.0)
                b_s = jnp.where(msk, pltpu.roll(bb, sh, axis=0), 0.0)
                bb = a * b_s + bb
                a = a * a_s
            h = a * hc + bb
            emit(off, h)
            return h[SUBLANES - 1:SUBLANES] if d == 0 else h[0:1]

        return lax.fori_loop(0, nb, body, carry, unroll=SCAN_UNROLL)

    def run(e_ref, n_all, d, carry, emit):
        cn = min(chunk, n_all)
        starts = list(range(0, n_all, cn))
        for r0 in (starts if d == 0 else starts[::-1]):
            gates(e_ref, r0, cn, d)
            carry = scan(cn, carry, d, functools.partial(emit, r0))
        return carry

    def emit_none(r0, off, h):
        pass

    def emit_set(r0, off, h):
        ybuf[pl.ds(r0 + off, SUBLANES), :] = h

    def emit_add(r0, off, h):
        ybuf[pl.ds(r0 + off, SUBLANES), :] += h

    h0 = jnp.zeros((1, c), F32)
    run(ext, t, 0, run(extc, tc, 0, h0, emit_none), emit_set)
    run(ext, t, 1, run(extc, tc, 1, h0, emit_none), emit_add)
    cn = min(chunk, t)
    for r0 in range(0, t, cn):
        o_ref[0, r0:r0 + cn, :] = (ybuf[r0:r0 + cn, :] * jax.nn.gelu(gate_ref[0, r0:r0 + cn, :])).astype(BF16)


def _rglru(p_lat, p_ctx, conv_w, conv_b, w_a, b_a, w_x, b_x, lam):
    b, t, c2 = p_lat.shape
    cw = c2 // 2
    tc = p_ctx.shape[1]
    nb = w_a.shape[1]
    bs = cw // nb
    chunk = 512
    vec = lambda rows: pl.BlockSpec((rows, bs), lambda bi, n: (0, n))
    wspec = pl.BlockSpec((2, 1, bs, bs), lambda bi, n: (0, n, 0, 0))
    return pl.pallas_call(
        functools.partial(_rglru_kernel, t=t, tc=tc, chunk=chunk),
        grid=(b, nb),
        in_specs=[pl.BlockSpec((1, t, bs), lambda bi, n: (bi, 0, n)),
                  pl.BlockSpec((1, t, bs), lambda bi, n: (bi, 0, nb + n)),
                  pl.BlockSpec((1, tc, bs), lambda bi, n: (bi, 0, n)),
                  vec(CONV_W), vec(1), wspec, vec(2), wspec, vec(2), vec(2)],
        out_specs=pl.BlockSpec((1, t, bs), lambda bi, n: (bi, 0, n)),
        out_shape=jax.ShapeDtypeStruct((b, t, cw), BF16),
        scratch_shapes=[pltpu.VMEM((t + 2 * CONV_PAD, bs), F32), pltpu.VMEM((tc + 2 * CONV_PAD, bs), F32),
                        pltpu.VMEM((min(chunk, max(t, tc)), bs), F32), pltpu.VMEM((min(chunk, max(t, tc)), bs), F32),
                        pltpu.VMEM((t, bs), F32)],
        compiler_params=_params(2),
        name="rglru",
    )(p_lat, p_lat, p_ctx, conv_w, conv_b.reshape(1, cw), w_a, b_a, w_x, b_x, lam)


def _lane_excl_cumsum(x, tri, ones):
    r, t = x.shape
    run = jnp.zeros((r, LANES), F32)
    outs = []
    for k in range(t // LANES):
        blk = x[:, k * LANES:(k + 1) * LANES]
        outs.append(_dot(blk, tri) + run)
        run = run + _dot(blk, ones)
    return jnp.concatenate(outs, axis=1)


def _route_kernel(lg_ref, idx_ref, gate_ref, rank_ref, kept_ref, *, t, cap):
    bi = pl.program_id(0)
    lg = lg_ref[0]
    ne = lg.shape[0]
    ex = jnp.exp(lg - jnp.max(lg, axis=0, keepdims=True))
    aff = ex / jnp.sum(ex, axis=0, keepdims=True)

    thr = jnp.zeros((ne, 1), jnp.int32)
    for bit in range(30, -1, -1):
        cand = thr | (1 << bit)
        n_ge = jnp.sum(jnp.where(aff >= lax.bitcast_convert_type(cand, F32), 1.0, 0.0), axis=1, keepdims=True)
        thr = jnp.where(n_ge >= cap, cand, thr)
    thr_f = lax.bitcast_convert_type(thr, F32)
    gt = aff > thr_f
    eq = aff == thr_f
    need = cap - jnp.sum(jnp.where(gt, 1.0, 0.0), axis=1, keepdims=True)

    ri = lax.broadcasted_iota(jnp.int32, (LANES, LANES), 0)
    ci = lax.broadcasted_iota(jnp.int32, (LANES, LANES), 1)
    tri = jnp.where(ri < ci, 1.0, 0.0).astype(BF16)
    ones = jnp.ones((LANES, LANES), BF16)
    eq_rank = _lane_excl_cumsum(jnp.where(eq, 1.0, 0.0).astype(BF16), tri, ones)
    sel = gt | (eq & (eq_rank < need))
    rank = _lane_excl_cumsum(jnp.where(sel, 1.0, 0.0).astype(BF16), tri, ones).astype(jnp.int32)
    kept = jnp.where(sel, rank, -1)
    rank_ref[0] = rank
    kept_ref[0] = kept

    tpos = lax.broadcasted_iota(jnp.int32, (1, t), 1)
    t_hi = (tpos >> 6).astype(F32)
    t_lo = (tpos & 63).astype(F32)
    g_hi = aff.astype(BF16).astype(F32)
    g_mid = (aff - g_hi).astype(BF16).astype(F32)
    g_lo = (aff - g_hi) - g_mid
    zrows = jnp.zeros((2 * SUBLANES - 5, t), F32)
    p_iota = lax.broadcasted_iota(jnp.int32, (cap, t), 0)
    for e in range(ne):
        onehot = jnp.where(kept[e:e + 1, :] == p_iota, 1.0, 0.0).astype(BF16)
        vals = jnp.concatenate([t_hi, t_lo, g_hi[e:e + 1], g_mid[e:e + 1], g_lo[e:e + 1], zrows], axis=0).astype(BF16)
        out = _dot_nt(vals, onehot)
        idx_ref[0, e:e + 1, :] = (out[0:1] * 64.0 + out[1:2]).astype(jnp.int32) + bi * t
        gate_ref[0, e:e + 1, :] = (out[2:3] + out[3:4]) + out[4:5]


def _route(lg_t):
    b, ne, t = lg_t.shape
    cap = EC_FACTOR * t // ne
    lst = lambda dt: jax.ShapeDtypeStruct((b, ne, cap), dt)
    dense = jax.ShapeDtypeStruct((b, ne, t), jnp.int32)
    per_b = lambda s: pl.BlockSpec((1,) + s, lambda bi: (bi, 0, 0))
    return pl.pallas_call(
        functools.partial(_route_kernel, t=t, cap=cap),
        grid=(b,),
        in_specs=[per_b((ne, t))],
        out_specs=[per_b((ne, cap)), per_b((ne, cap)), per_b((ne, t)), per_b((ne, t))],
        out_shape=[lst(jnp.int32), lst(F32), dense, dense],
        compiler_params=_params(1),
        name="moe_route",
    )(lg_t)


def _window_tables(rank, cap, tm):
    b, ne, t = rank.shape
    lo = jnp.swapaxes(rank[:, :, ::tm], 1, 2)
    hi = jnp.concatenate([lo[:, 1:], jnp.full((b, 1, ne), cap, jnp.int32)], axis=1)
    span = hi - lo // SUBLANES * SUBLANES
    step = MOE_WINDOW - SUBLANES
    sweeps = jnp.maximum(jnp.max((span + step - 1) // step, axis=2), 1)
    return lo.reshape(-1), sweeps.reshape(-1)


def _moe_ffn_kernel(*refs, n_rt, part_rows):
    n_src = len(part_rows)
    idx_ref, idxn_ref, g_ref = refs[:3]
    h_refs = refs[3:3 + n_src]
    w1_ref, w3_ref, w2_ref = refs[3 + n_src:6 + n_src]
    y_refs = refs[6 + n_src:6 + 2 * n_src]
    stage, xb, acc, sem = refs[6 + 2 * n_src:]
    tr = sum(part_rows)
    f = pl.program_id(2)
    tile = pl.program_id(0) * n_rt + pl.program_id(1)
    n_tiles = pl.num_programs(0) * n_rt
    starts = [sum(part_rows[:i]) for i in range(n_src)]

    def gather(ix_ref):
        for lo, n, h_ref in zip(starts, part_rows, h_refs):
            def issue(p, c, h_ref=h_ref):
                pltpu.make_async_copy(h_ref.at[pl.ds(ix_ref[0, 0, p], 1)], stage.at[pl.ds(p, 1)], sem).start()
                return c
            lax.fori_loop(lo, lo + n, issue, 0, unroll=DMA_ISSUE_UNROLL)

    @pl.when(f == 0)
    def _():
        @pl.when(tile == 0)
        def _():
            gather(idx_ref)
        pltpu.make_async_copy(h_refs[0].at[pl.ds(0, tr)], stage, sem).wait()
        xb[...] = stage[...].astype(BF16)
        acc[...] = jnp.zeros_like(acc)

        @pl.when(tile + 1 < n_tiles)
        def _():
            gather(idxn_ref)

    x = xb[...]
    a1 = _dot(x, w1_ref[0, 0].astype(BF16))
    a3 = _dot(x, w3_ref[0, 0].astype(BF16))
    hmid = (a1 * jax.nn.sigmoid(a1) * a3).astype(BF16)
    acc[...] += _dot(hmid, w2_ref[0, 0].astype(BF16))

    @pl.when(f == pl.num_programs(2) - 1)
    def _():
        for lo, n, y_ref in zip(starts, part_rows, y_refs):
            y_ref[0] = acc[lo:lo + n, :] * g_ref[0, lo:lo + n, :]


def _moe_ffn(idx, gates, hs, w1, w3, w2, layer, part_rows):
    e, n_rt, tr = idx.shape
    d = hs[0].shape[-1]
    ff = w1.shape[-1]
    tf = _row_tile(ff, 256)
    n_src = len(hs)
    tiles = lambda a: a.reshape(e * n_rt, 1, tr)
    smem = lambda imap: pl.BlockSpec((1, 1, tr), imap, memory_space=pltpu.SMEM)
    cur = lambda ei, ri, f: (ei * n_rt + ri, 0, 0)
    nxt = lambda ei, ri, f: (jnp.minimum(ei * n_rt + ri + 1, e * n_rt - 1), 0, 0)
    any_spec = pl.BlockSpec(memory_space=pl.ANY)
    return pl.pallas_call(
        functools.partial(_moe_ffn_kernel, n_rt=n_rt, part_rows=tuple(part_rows)),
        grid=(e, n_rt, ff // tf),
        in_specs=[smem(cur), smem(nxt), pl.BlockSpec((1, tr, 1), lambda ei, ri, f: (ei, ri, 0))]
                 + [any_spec] * n_src
                 + [pl.BlockSpec((1, 1, d, tf), lambda ei, ri, f: (layer, ei, 0, f)),
                    pl.BlockSpec((1, 1, d, tf), lambda ei, ri, f: (layer, ei, 0, f)),
                    pl.BlockSpec((1, 1, tf, d), lambda ei, ri, f: (layer, ei, f, 0))],
        out_specs=[pl.BlockSpec((1, n, d), lambda ei, ri, f: (ei, ri, 0)) for n in part_rows],
        out_shape=[jax.ShapeDtypeStruct((e, n * n_rt, d), F32) for n in part_rows],
        scratch_shapes=[pltpu.VMEM((tr, d), F32), pltpu.VMEM((tr, d), BF16), pltpu.VMEM((tr, d), F32),
                        pltpu.SemaphoreType.DMA],
        compiler_params=_params(3),
        name="moe_ffn",
    )(tiles(idx), tiles(idx), gates.reshape(e, n_rt * tr, 1), *hs, w1, w3, w2)


def _combine_kernel(lo_ref, sw_ref, x_ref, gt_ref, kept_ref, y_ref, fg_ref, o_ref, ybuf, acc, sem, *, cap, final):
    bi = pl.program_id(0)
    step = bi * pl.num_programs(1) + pl.program_id(1)
    n_steps = pl.num_programs(0) * pl.num_programs(1)
    ne = kept_ref.shape[2]
    w = MOE_WINDOW
    al = SUBLANES
    rows_all = y_ref.shape[1]

    def window_start(st, e, c):
        first = (st // pl.num_programs(1)) * cap + lo_ref[st * ne + e] // al * al + c * (w - al)
        return pl.multiple_of(jnp.minimum(first, rows_all - w), al)

    def fetch(st, c, slot, e):
        return pltpu.make_async_copy(y_ref.at[e, pl.ds(window_start(st, e, c), w)],
                                     ybuf.at[slot, pl.ds(e * w, w)], sem.at[slot])

    def start_all(st, c, slot):
        for e in range(ne):
            fetch(st, c, slot, e).start()

    def wait_all(st, c, slot):
        for e in range(ne):
            fetch(st, c, slot, e).wait()

    slot = lax.rem(step, 2)

    @pl.when(step == 0)
    def _():
        start_all(step, 0, slot)

    wait_all(step, 0, slot)

    @pl.when(step + 1 < n_steps)
    def _():
        start_all(step + 1, 0, 1 - slot)

    kept = kept_ref[0]
    lane = lax.broadcasted_iota(jnp.int32, (1, 2 * w), 1)
    first_half = lane < w
    row_in = jnp.where(first_half, lane, lane - w)
    acc[...] = jnp.zeros_like(acc)

    def add_windows(c, sl):
        span = jnp.where(c == sw_ref[step] - 1, rows_all, w - al)
        pieces = []
        for e2 in range(0, ne, 2):
            pos = jnp.where(first_half, kept[:, e2:e2 + 1], kept[:, e2 + 1:e2 + 2])
            off = jnp.where(first_half, window_start(step, e2, c), window_start(step, e2 + 1, c)) - bi * cap
            own_lo = jnp.where(first_half, lo_ref[step * ne + e2] // al * al,
                               lo_ref[step * ne + e2 + 1] // al * al) + c * (w - al)
            hit = (pos - off == row_in) & (pos >= own_lo) & (pos < own_lo + span)
            pieces.append(jnp.where(hit, 1.0, 0.0).astype(BF16))
        own = jnp.concatenate(pieces, axis=1)
        acc[...] += _dot(own, ybuf[sl].astype(BF16))

    add_windows(0, slot)

    def extra(c, carry):
        start_all(step, c, slot)
        wait_all(step, c, slot)
        add_windows(c, slot)
        return carry

    lax.fori_loop(1, sw_ref[step], extra, 0)
    xn = x_ref[0] + gt_ref[0] * acc[...]
    o_ref[0] = _rms_scale(xn) * fg_ref[...] if final else xn


def _combine(x, gt, kept_t, y, tables, final_g, final):
    b, t, d = x.shape
    ne = kept_t.shape[2]
    cap = EC_FACTOR * t // ne
    tm = _row_tile(t, MOE_TOKEN_TILE)
    lo, sweeps = tables
    tok = lambda wd: pl.BlockSpec((1, tm, wd), lambda bi, i, *_: (bi, i, 0))
    return pl.pallas_call(
        functools.partial(_combine_kernel, cap=cap, final=final),
        grid_spec=pltpu.PrefetchScalarGridSpec(
            num_scalar_prefetch=2,
            grid=(b, t // tm),
            in_specs=[tok(d), pl.BlockSpec((1, 1, d), lambda bi, i, *_: (bi, 0, 0)), tok(ne),
                      pl.BlockSpec(memory_space=pl.ANY), pl.BlockSpec(final_g.shape, lambda bi, i, *_: (0, 0))],
            out_specs=tok(d),
            scratch_shapes=[pltpu.VMEM((2, ne * MOE_WINDOW, d), F32), pltpu.VMEM((tm, d), F32),
                            pltpu.SemaphoreType.DMA((2,))]),
        out_shape=jax.ShapeDtypeStruct((b, t, d), F32),
        compiler_params=_params(2),
        name="moe_combine",
    )(lo, sweeps, x, gt, kept_t, y, final_g)


def _rope_tables(t):
    rows = t // GRID_W
    row = jnp.repeat(jnp.arange(rows, dtype=F32), GRID_W)
    col = jnp.tile(jnp.arange(GRID_W, dtype=F32), rows)
    inv = ROPE_BASE ** (-jnp.arange(0, ROPE_AXIS, 2, dtype=F32) / ROPE_AXIS)
    ar = row[:, None] * inv
    ac = col[:, None] * inv
    cos = jnp.concatenate([jnp.cos(ar), jnp.cos(ar), jnp.cos(ac), jnp.cos(ac)], axis=-1)
    sin = jnp.concatenate([jnp.sin(ar), jnp.sin(ar), jnp.sin(ac), jnp.sin(ac)], axis=-1)
    return jnp.concatenate([cos, sin], axis=-1)


def _rot_columns(w):
    half = ROPE_AXIS // 2
    return jnp.concatenate([-w[..., half:ROPE_AXIS], w[..., :half],
                            -w[..., ROPE_AXIS + half:], w[..., ROPE_AXIS:ROPE_AXIS + half]], axis=-1)


def _moe(streams, w1, w3, w2, layer):
    ne = w1.shape[1]
    n_rt = MOE_ROW_TILES
    idxs, gates, hs, infos = [], [], [], []
    for h, lg_t in streams:
        b, t, d = h.shape
        idx, gate, rank, kept = _route(lg_t)
        by_tile = lambda a: jnp.swapaxes(a, 0, 1).reshape(ne, n_rt, -1)
        idxs.append(by_tile(idx))
        gates.append(by_tile(gate))
        hs.append(h.reshape(b * t, d))
        infos.append((jnp.swapaxes(kept, 1, 2), _window_tables(rank, EC_FACTOR * t // ne, _row_tile(t, MOE_TOKEN_TILE))))
    ys = _moe_ffn(jnp.concatenate(idxs, axis=2), jnp.concatenate(gates, axis=2), hs, w1, w3, w2, layer,
                  [a.shape[2] for a in idxs])
    return [(y,) + info for y, info in zip(ys, infos)]


def kernel(x, c, ctx, c_ctx, w_ada, b_ada, norm_g, final_g, w_in_e, g_qa, w_uq, g_kv, w_ukv, w_pool, pool_scale,
           w_out_e, w_in_o, conv_w, conv_b, w_rg_a, b_rg_a, w_rg_x, b_rg_x, lru_lambda, w_out_o, w_router, w_e1,
           w_e3, w_e2):
    b, t, d = x.shape
    tc = ctx.shape[1]
    depth = w_ada.shape[0]
    n_heads = d // (2 * V_HEAD)
    pool_w = d // 2

    c_rows = jnp.concatenate([c, c_ctx[None], jnp.zeros((SUBLANES - b - 1, d), F32)], axis=0)
    mods = _mods(c_rows, w_ada, b_ada)

    def mod_rows(layer, k):
        m = mods[layer, :, k * d:(k + 1) * d]
        return m[:b, None, :], jnp.broadcast_to(m[b][None, None, :], (b, 1, d))

    cs_lat = _rope_tables(t)
    cs_ctx = jnp.concatenate([jnp.ones((tc, QK_ROPE), F32), jnp.zeros((tc, QK_ROPE), F32)], axis=-1)

    cs = ctx
    for layer in range(depth):
        last = layer == depth - 1
        (sh_m, csh_m), (sc_m, csc_m), (gt_m, cgt_m), (sh_f, csh_f), (sc_f, csc_f), (gt_f, cgt_f) = (
            mod_rows(layer, k) for k in range(N_MOD))
        g1 = norm_g[layer, 0][None]
        g2 = norm_g[layer, 1][None]
        if layer % 2 == 0:
            e = layer // 2
            wi = w_in_e[e]
            o_kpe = Q_LORA + KV_LORA
            kpe = wi[:, o_kpe:o_kpe + QK_ROPE]
            win = jnp.concatenate([wi[:, :o_kpe], wi[:, o_kpe + QK_ROPE:], kpe, _rot_columns(kpe)], axis=1).astype(BF16)
            wq3 = w_uq[e].reshape(Q_LORA, n_heads, QK_HEAD)
            wq = jnp.concatenate([wq3, _rot_columns(wq3[..., QK_NOPE:])], axis=-1).reshape(Q_LORA, n_heads * HEAD_PAD)
            wkv3 = w_ukv[e].reshape(KV_LORA, n_heads, QK_NOPE + V_HEAD)
            wk = wkv3[..., :QK_NOPE].reshape(KV_LORA, n_heads * QK_NOPE).astype(BF16)
            wv = wkv3[..., QK_NOPE:].reshape(KV_LORA, n_heads * V_HEAD).astype(BF16)
            pre = functools.partial(_even_pre, g=g1, win=win, gqa=g_qa[e][None], gkv=g_kv[e][None],
                                    wq=wq.astype(BF16), wk=wk, wv=wv, n_heads=n_heads)
            q_l, k_l, v_l, u_l = pre(x, sc_m, sh_m, cs_tab=cs_lat)
            q_c, k_c, v_c, u_c = pre(cs, csc_m, csh_m, cs_tab=cs_ctx)
            wo = w_out_e[e].astype(BF16)
            wo_a, wo_p = wo[:n_heads * V_HEAD], wo[n_heads * V_HEAD:]
            a_l = _attention(q_l, [(k_c, v_c), (k_l, v_l)], n_heads)
            pl_l = _pool(u_l, w_pool[e].astype(BF16), pool_scale[e][None])
            x, h_l, lg_l = _mix_out([a_l, pl_l], [wo_a, wo_p], x, gt_m, g2, sc_f, sh_f, w_router[layer].T)
            streams = [(h_l, lg_l)]
            if not last:
                a_c = _attention(q_c, [(k_c, v_c)], n_heads)
                pl_c = _pool(u_c, w_pool[e].astype(BF16), pool_scale[e][None])
                cs, h_c, lg_c = _mix_out([a_c, pl_c], [wo_a, wo_p], cs, cgt_m, g2, csc_f, csh_f, w_router[layer].T)
                streams.append((h_c, lg_c))
        else:
            o = layer // 2
            wi = w_in_o[o].astype(BF16)
            lw = wi.shape[1] // 2
            p_l = _norm_matmul(x, sc_m, sh_m, g1, wi)
            p_c = _norm_matmul(cs, csc_m, csh_m, g1, wi if not last else wi[:, lw:])
            wo = w_out_o[o].astype(BF16)
            y_l = _rglru(p_l, p_c[..., p_c.shape[-1] - lw:], conv_w[o], conv_b[o], w_rg_a[o].astype(BF16), b_rg_a[o],
                         w_rg_x[o].astype(BF16), b_rg_x[o], lru_lambda[o])
            x, h_l, lg_l = _mix_out([y_l], [wo], x, gt_m, g2, sc_f, sh_f, w_router[layer].T)
            streams = [(h_l, lg_l)]
            if not last:
                raise NotImplementedError("context output of an RG-LRU layer is only needed when it is not the last layer")
        outs = _moe(streams, w_e1, w_e3, w_e2, layer)
        y_l, kept_l, tab_l = outs[0]
        x = _combine(x, gt_f, kept_l, y_l, tab_l, final_g[None], final=last)
        if not last:
            y_c, kept_c, tab_c = outs[1]
            cs = _combine(cs, cgt_f, kept_c, y_c, tab_c, final_g[None], final=False)
    return x
```

```python
import functools
import math

import jax
import jax.numpy as jnp
from jax import lax
from jax.experimental import pallas as pl
from jax.experimental.pallas import tpu as pltpu

F32 = jnp.float32
BF16 = jnp.bfloat16

RMS_EPS = 1e-6
N_MOD = 6
GRID_W = 64
ROPE_BASE = 10000.0

V_HEAD = 128
QK_NOPE = 128
QK_ROPE = 64
QK_HEAD = QK_NOPE + QK_ROPE
Q_LORA = 512
KV_LORA = 512
ROPE_AXIS = QK_ROPE // 2
HEAD_PAD = 256
KEY_CHUNK = 256

POOL_WINDOWS = (2, 4, 8, 16)
POOL_PAD = 16

LRU_BLOCKS = 8
CONV_W = 4
CONV_PAD = 8
LRU_C = 8.0

N_EXPERTS = 16
EC_FACTOR = 2
MOE_ROW_TILES = 2
MOE_TOKEN_TILE = 256
MOE_WINDOW = 64

DMA_ISSUE_UNROLL = 8
SCAN_UNROLL = 8

SUBLANES = 8
LANES = 128
VMEM_LIMIT_BYTES = 56 * 1024 * 1024


def _params(n_grid):
    return pltpu.CompilerParams(dimension_semantics=("arbitrary",) * n_grid,
                                vmem_limit_bytes=VMEM_LIMIT_BYTES)


def _const_spec(shape):
    nd = len(shape)
    return pl.BlockSpec(shape, lambda *_: (0,) * nd, pipeline_mode=pl.Buffered(1))


def _rms_scale(xf):
    return xf * lax.rsqrt(jnp.mean(xf * xf, axis=-1, keepdims=True) + RMS_EPS)


def _dot(a, b):
    return jnp.dot(a, b, preferred_element_type=F32)


def _dot_nt(a, b):
    return lax.dot_general(a, b, (((1,), (1,)), ((), ())), preferred_element_type=F32)


def _row_tile(t, pref):
    return pref if t % pref == 0 else t


def _mods_kernel(c_ref, w_ref, b_ref, o_ref):
    c = c_ref[...]
    s = c * jax.nn.sigmoid(c)
    o_ref[0] = _dot(s.astype(BF16), w_ref[0].astype(BF16)) + b_ref[0]


def _mods(c_rows, w_ada, b_ada):
    depth, d, n = w_ada.shape
    tn = 1024
    return pl.pallas_call(
        _mods_kernel,
        grid=(depth, n // tn),
        in_specs=[_const_spec(c_rows.shape),
                  pl.BlockSpec((1, d, tn), lambda l, j: (l, 0, j)),
                  pl.BlockSpec((1, 1, tn), lambda l, j: (l, 0, j))],
        out_specs=pl.BlockSpec((1, c_rows.shape[0], tn), lambda l, j: (l, 0, j)),
        out_shape=jax.ShapeDtypeStruct((depth, c_rows.shape[0], n), F32),
        compiler_params=_params(2),
        name="adaln_mods",
    )(c_rows, w_ada, b_ada.reshape(depth, 1, n))


def _even_pre_kernel(x_ref, sc_ref, sh_ref, g_ref, win_ref, gqa_ref, gkv_ref, wq_ref, wk_ref, wv_ref, cs_ref,
                     q_ref, k_ref, v_ref, u_ref, *, n_heads, scale, pool_w):
    h = _rms_scale(x_ref[0]) * (g_ref[...] * (1.0 + sc_ref[0])) + sh_ref[0]
    p = _dot(h.astype(BF16), win_ref[...])
    cs = cs_ref[...]
    o_pool = Q_LORA + KV_LORA
    o_rope = o_pool + pool_w

    ql = _rms_scale(p[:, :Q_LORA]) * gqa_ref[...]
    q = _dot(ql.astype(BF16), wq_ref[...])
    for hd in range(n_heads):
        base = hd * HEAD_PAD
        hi = q[:, base + QK_NOPE:base + HEAD_PAD] * cs
        hi = hi + pltpu.roll(hi, QK_ROPE, axis=1)
        q_ref[0, :, base:base + QK_NOPE] = (q[:, base:base + QK_NOPE] * scale).astype(BF16)
        q_ref[0, :, base + QK_NOPE:base + HEAD_PAD] = (hi * scale).astype(BF16)

    kvl = (_rms_scale(p[:, Q_LORA:o_pool]) * gkv_ref[...]).astype(BF16)
    kn = _dot(kvl, wk_ref[...])
    v_ref[0] = _dot(kvl, wv_ref[...]).astype(BF16)
    kp = p[:, o_rope:o_rope + 2 * QK_ROPE] * cs
    kp = kp + pltpu.roll(kp, QK_ROPE, axis=1)
    lane = lax.broadcasted_iota(jnp.int32, kp.shape, 1)
    kp = jnp.where(lane < QK_ROPE, kp, 0.0).astype(BF16)
    for hd in range(n_heads):
        base = hd * HEAD_PAD
        k_ref[0, :, base:base + QK_NOPE] = kn[:, hd * QK_NOPE:(hd + 1) * QK_NOPE].astype(BF16)
        k_ref[0, :, base + QK_NOPE:base + HEAD_PAD] = kp
    u_ref[0] = p[:, o_pool:o_rope]


def _even_pre(x, sc, sh, g, win, gqa, gkv, wq, wk, wv, cs_tab, n_heads):
    b, t, d = x.shape
    tm = _row_tile(t, 256)
    pool_w = d // 2
    kern = functools.partial(_even_pre_kernel, n_heads=n_heads, scale=QK_HEAD ** -0.5 * math.log2(math.e),
                             pool_w=pool_w)
    tok = lambda w: pl.BlockSpec((1, tm, w), lambda bi, i: (bi, i, 0))
    per_b = pl.BlockSpec((1, 1, d), lambda bi, i: (bi, 0, 0))
    return pl.pallas_call(
        kern,
        grid=(b, t // tm),
        in_specs=[tok(d), per_b, per_b, _const_spec(g.shape), _const_spec(win.shape), _const_spec(gqa.shape),
                  _const_spec(gkv.shape), _const_spec(wq.shape), _const_spec(wk.shape), _const_spec(wv.shape),
                  pl.BlockSpec((tm, 2 * QK_ROPE), lambda bi, i: (i, 0))],
        out_specs=[tok(n_heads * HEAD_PAD), tok(n_heads * HEAD_PAD), tok(n_heads * V_HEAD), tok(pool_w)],
        out_shape=[jax.ShapeDtypeStruct((b, t, n_heads * HEAD_PAD), BF16),
                   jax.ShapeDtypeStruct((b, t, n_heads * HEAD_PAD), BF16),
                   jax.ShapeDtypeStruct((b, t, n_heads * V_HEAD), BF16),
                   jax.ShapeDtypeStruct((b, t, pool_w), F32)],
        compiler_params=_params(2),
        name="even_pre",
    )(x, sc, sh, g, win, gqa, gkv, wq, wk, wv, cs_tab)


def _attn_kernel(*refs, chunks):
    q_ref, o_ref = refs[0], refs[-1]
    q = q_ref[0]
    m = l = acc = None
    for i, s0, n in chunks:
        s = _dot_nt(q, refs[1 + 2 * i][0, s0:s0 + n, :])
        v = refs[2 + 2 * i][0, s0:s0 + n, :]
        mc = jnp.max(s, axis=-1, keepdims=True)
        if m is None:
            m_new = mc
            p = jnp.exp2(s - m_new)
            l = jnp.sum(p, axis=-1, keepdims=True)
            acc = _dot(p.astype(BF16), v)
        else:
            m_new = jnp.maximum(m, mc)
            alpha = jnp.exp2(m - m_new)
            p = jnp.exp2(s - m_new)
            l = alpha * l + jnp.sum(p, axis=-1, keepdims=True)
            acc = alpha * acc + _dot(p.astype(BF16), v)
        m = m_new
    o_ref[0] = (acc * (1.0 / l)).astype(BF16)


def _attention(q, kvs, n_heads):
    b, tq_all, _ = q.shape
    tq = _row_tile(tq_all, 2048)
    in_specs = [pl.BlockSpec((1, tq, HEAD_PAD), lambda bi, hd, i: (bi, i, hd))]
    args = [q]
    chunks = []
    for j, (k, v) in enumerate(kvs):
        s = k.shape[1]
        tk = _row_tile(s, KEY_CHUNK)
        chunks += [(j, s0, tk) for s0 in range(0, s, tk)]
        in_specs.append(pl.BlockSpec((1, s, HEAD_PAD), lambda bi, hd, i: (bi, 0, hd)))
        in_specs.append(pl.BlockSpec((1, s, V_HEAD), lambda bi, hd, i: (bi, 0, hd)))
        args += [k, v]
    return pl.pallas_call(
        functools.partial(_attn_kernel, chunks=tuple(chunks)),
        grid=(b, n_heads, tq_all // tq),
        in_specs=in_specs,
        out_specs=pl.BlockSpec((1, tq, V_HEAD), lambda bi, hd, i: (bi, i, hd)),
        out_shape=jax.ShapeDtypeStruct((b, tq_all, n_heads * V_HEAD), BF16),
        compiler_params=_params(3),
        name="mla_attention",
    )(*args)


def _pool_kernel(u_ref, w_ref, ps_ref, o_ref, e0, e1, *, t, windows):
    gi = pl.program_id(1)
    gc = u_ref.shape[-1]
    u = u_ref[0]
    zpad = jnp.zeros((POOL_PAD, gc), F32)
    for e in (e0, e1):
        e[0:POOL_PAD, :] = zpad
        e[POOL_PAD + t:2 * POOL_PAD + t, :] = zpad
    e0[POOL_PAD:POOL_PAD + t, :] = u
    pos = lax.broadcasted_iota(jnp.int32, (t, 1), 0)
    n = t + 2 * POOL_PAD - 2 * SUBLANES
    for k, w in enumerate(windows):
        @pl.when(gi == k)
        def _(w=w):
            src, dst = e0, e1
            dst[SUBLANES:SUBLANES + n, :] = src[SUBLANES - 1:SUBLANES - 1 + n, :] + src[SUBLANES:SUBLANES + n, :]
            src, dst = dst, src
            m = 2
            while m < w:
                hm = m // 2
                dst[SUBLANES:SUBLANES + n, :] = (src[SUBLANES - hm:SUBLANES - hm + n, :]
                                                 + src[SUBLANES + hm:SUBLANES + hm + n, :])
                src, dst = dst, src
                m *= 2
            tot = src[POOL_PAD:POOL_PAD + t, :]
            cnt = (jnp.minimum(pos + (w - w // 2), t) - jnp.maximum(pos - w // 2, 0)).astype(F32)
            pooled = tot / cnt - u
            o_ref[0] = (_dot(pooled.astype(BF16), w_ref[0]) * ps_ref[...]).astype(BF16)


def _pool(u, w_pool, pool_scale):
    b, t, pw = u.shape
    ng = len(POOL_WINDOWS)
    gc = pw // ng
    return pl.pallas_call(
        functools.partial(_pool_kernel, t=t, windows=POOL_WINDOWS),
        grid=(b, ng),
        in_specs=[pl.BlockSpec((1, t, gc), lambda bi, g: (bi, 0, g)),
                  pl.BlockSpec((1, gc, gc), lambda bi, g: (g, 0, 0)),
                  pl.BlockSpec((1, gc), lambda bi, g: (0, g))],
        out_specs=pl.BlockSpec((1, t, gc), lambda bi, g: (bi, 0, g)),
        out_shape=jax.ShapeDtypeStruct((b, t, pw), BF16),
        scratch_shapes=[pltpu.VMEM((t + 2 * POOL_PAD, gc), F32), pltpu.VMEM((t + 2 * POOL_PAD, gc), F32)],
        compiler_params=_params(2),
        name="pool_mixer",
    )(u, w_pool, pool_scale)


def _mix_out_kernel(*refs, n_a):
    a_refs, w_refs = refs[:n_a], refs[n_a:2 * n_a]
    x_ref, gt_ref, g2_ref, sc_ref, sh_ref, wr_ref, xo_ref, h_ref, lg_ref = refs[2 * n_a:]
    m = functools.reduce(jnp.add, [_dot(a_refs[i][0], w_refs[i][...]) for i in range(n_a)])
    xn = x_ref[0] + gt_ref[0] * m
    xo_ref[0] = xn
    hf = _rms_scale(xn) * (g2_ref[...] * (1.0 + sc_ref[0])) + sh_ref[0]
    h_ref[0] = hf
    hb = hf.astype(BF16)
    h_lo = (hf - hb.astype(F32)).astype(BF16)
    wr = wr_ref[...]
    w_hi = wr.astype(BF16)
    w_lo = (wr - w_hi.astype(F32)).astype(BF16)
    lg_ref[0] = _dot_nt(w_hi, hb) + (_dot_nt(w_lo, hb) + _dot_nt(w_hi, h_lo))


def _mix_out(a_list, w_list, x, gt, g2, sc, sh, w_router_t):
    b, t, d = x.shape
    tm = _row_tile(t, 512)
    ne = w_router_t.shape[0]
    tok = lambda w: pl.BlockSpec((1, tm, w), lambda bi, i: (bi, i, 0))
    per_b = pl.BlockSpec((1, 1, d), lambda bi, i: (bi, 0, 0))
    in_specs = ([tok(a.shape[-1]) for a in a_list] + [_const_spec(w.shape) for w in w_list]
                + [tok(d), per_b, _const_spec(g2.shape), per_b, per_b, _const_spec(w_router_t.shape)])
    return pl.pallas_call(
        functools.partial(_mix_out_kernel, n_a=len(a_list)),
        grid=(b, t // tm),
        in_specs=in_specs,
        out_specs=[tok(d), tok(d), pl.BlockSpec((1, ne, tm), lambda bi, i: (bi, 0, i))],
        out_shape=[jax.ShapeDtypeStruct((b, t, d), F32), jax.ShapeDtypeStruct((b, t, d), F32),
                   jax.ShapeDtypeStruct((b, ne, t), F32)],
        compiler_params=_params(2),
        name="mix_out",
    )(*a_list, *w_list, x, gt, g2, sc, sh, w_router_t)


def _norm_matmul_kernel(x_ref, sc_ref, sh_ref, g_ref, w_ref, o_ref):
    h = _rms_scale(x_ref[0]) * (g_ref[...] * (1.0 + sc_ref[0])) + sh_ref[0]
    o_ref[0] = _dot(h.astype(BF16), w_ref[...])


def _norm_matmul(x, sc, sh, g, w):
    b, t, d = x.shape
    n = w.shape[1]
    tm = _row_tile(t, 512)
    tn = _row_tile(n, 2048)
    return pl.pallas_call(
        _norm_matmul_kernel,
        grid=(n // tn, b, t // tm),
        in_specs=[pl.BlockSpec((1, tm, d), lambda j, bi, i: (bi, i, 0)),
                  pl.BlockSpec((1, 1, d), lambda j, bi, i: (bi, 0, 0)),
                  pl.BlockSpec((1, 1, d), lambda j, bi, i: (bi, 0, 0)),
                  _const_spec(g.shape),
                  pl.BlockSpec((d, tn), lambda j, bi, i: (0, j))],
        out_specs=pl.BlockSpec((1, tm, tn), lambda j, bi, i: (bi, i, j)),
        out_shape=jax.ShapeDtypeStruct((b, t, n), F32),
        compiler_params=_params(3),
        name="norm_matmul",
    )(x, sc, sh, g, w)


def _rglru_kernel(gate_ref, rec_ref, recc_ref, cw_ref, cb_ref, wa_ref, ba_ref, wx_ref, bx_ref, lam_ref, o_ref,
                  ext, extc, ubuf, ubufc, abuf, bbuf, ybuf, *, t, tc, chunk):
    c = rec_ref.shape[-1]
    zpad = jnp.zeros((CONV_PAD, c), F32)
    for e, r, n in ((ext, rec_ref, t), (extc, recc_ref, tc)):
        e[0:CONV_PAD, :] = zpad
        e[CONV_PAD + n:2 * CONV_PAD + n, :] = zpad
        e[CONV_PAD:CONV_PAD + n, :] = r[0]
    cw = cw_ref[...]
    cb = cb_ref[...]
    row = lax.broadcasted_iota(jnp.int32, (SUBLANES, c), 0)

    def gates(e_ref, u_ref, r0, n, d):
        if d == 0:
            u = cb
            for k in range(CONV_W):
                off = CONV_PAD + r0 + k - CONV_W // 2
                u = u + cw[k:k + 1] * e_ref[off:off + n, :]
            u_ref[r0:r0 + n, :] = u
        else:
            u = u_ref[r0:r0 + n, :]
        ub = u.astype(BF16)
        r = jax.nn.sigmoid(_dot(ub, wa_ref[d, 0]) + ba_ref[d:d + 1])
        i = jax.nn.sigmoid(_dot(ub, wx_ref[d, 0]) + bx_ref[d:d + 1])
        log_a = (-LRU_C * r) * jax.nn.softplus(-lam_ref[d:d + 1])
        a = jnp.exp(log_a)
        abuf[0:n, :] = a
        bbuf[0:n, :] = jnp.sqrt(1.0 - a * a) * (i * u)

    def scan(n, carry, d, emit):
        nb = n // SUBLANES

        def body(j, hc):
            jj = j if d == 0 else nb - 1 - j
            off = pl.multiple_of(jj * SUBLANES, SUBLANES)
            a = abuf[pl.ds(off, SUBLANES), :]
            bb = bbuf[pl.ds(off, SUBLANES), :]
            for s in (1, 2, 4):
                if d == 0:
                    sh, msk = s, row >= s
                else:
                    sh, msk = SUBLANES - s, row < SUBLANES - s
                a_s = jnp.where(msk, pltpu.roll(a, sh, axis=0), 1.0)
                b_s = jnp.where(msk, pltpu.roll(bb, sh, axis=0), 0.0)
                bb = a * b_s + bb
                a = a * a_s
            h = a * hc + bb
            emit(off, h)
            return h[SUBLANES - 1:SUBLANES] if d == 0 else h[0:1]

        return lax.fori_loop(0, nb, body, carry, unroll=SCAN_UNROLL)

    def run(e_ref, u_ref, n_all, d, carry, emit):
        cn = min(chunk, n_all)
        starts = list(range(0, n_all, cn))
        for r0 in (starts if d == 0 else starts[::-1]):
            gates(e_ref, u_ref, r0, cn, d)
            carry = scan(cn, carry, d, functools.partial(emit, r0))
        return carry

    def emit_none(r0, off, h):
        pass

    def emit_set(r0, off, h):
        ybuf[pl.ds(r0 + off, SUBLANES), :] = h

    def emit_add(r0, off, h):
        ybuf[pl.ds(r0 + off, SUBLANES), :] += h

    h0 = jnp.zeros((1, c), F32)
    run(ext, ubuf, t, 0, run(extc, ubufc, tc, 0, h0, emit_none), emit_set)
    run(ext, ubuf, t, 1, run(extc, ubufc, tc, 1, h0, emit_none), emit_add)
    cn = min(chunk, t)
    for r0 in range(0, t, cn):
        o_ref[0, r0:r0 + cn, :] = (ybuf[r0:r0 + cn, :] * jax.nn.gelu(gate_ref[0, r0:r0 + cn, :])).astype(BF16)


def _rglru(p_lat, p_ctx, conv_w, conv_b, w_a, b_a, w_x, b_x, lam):
    b, t, c2 = p_lat.shape
    cw = c2 // 2
    tc = p_ctx.shape[1]
    nb = w_a.shape[1]
    bs = cw // nb
    chunk = 512
    vec = lambda rows: pl.BlockSpec((rows, bs), lambda bi, n: (0, n))
    wspec = pl.BlockSpec((2, 1, bs, bs), lambda bi, n: (0, n, 0, 0))
    return pl.pallas_call(
        functools.partial(_rglru_kernel, t=t, tc=tc, chunk=chunk),
        grid=(b, nb),
        in_specs=[pl.BlockSpec((1, t, bs), lambda bi, n: (bi, 0, n)),
                  pl.BlockSpec((1, t, bs), lambda bi, n: (bi, 0, nb + n)),
                  pl.BlockSpec((1, tc, bs), lambda bi, n: (bi, 0, n)),
                  vec(CONV_W), vec(1), wspec, vec(2), wspec, vec(2), vec(2)],
        out_specs=pl.BlockSpec((1, t, bs), lambda bi, n: (bi, 0, n)),
        out_shape=jax.ShapeDtypeStruct((b, t, cw), BF16),
        scratch_shapes=[pltpu.VMEM((t + 2 * CONV_PAD, bs), F32), pltpu.VMEM((tc + 2 * CONV_PAD, bs), F32),
                        pltpu.VMEM((t, bs), F32), pltpu.VMEM((tc, bs), F32),
                        pltpu.VMEM((min(chunk, max(t, tc)), bs), F32), pltpu.VMEM((min(chunk, max(t, tc)), bs), F32),
                        pltpu.VMEM((t, bs), F32)],
        compiler_params=_params(2),
        name="rglru",
    )(p_lat, p_lat, p_ctx, conv_w, conv_b.reshape(1, cw), w_a, b_a, w_x, b_x, lam)


def _lane_excl_cumsum(x, tri, ones):
    r, t = x.shape
    run = jnp.zeros((r, LANES), F32)
    outs = []
    for k in range(t // LANES):
        blk = x[:, k * LANES:(k + 1) * LANES]
        outs.append(_dot(blk, tri) + run)
        run = run + _dot(blk, ones)
    return jnp.concatenate(outs, axis=1)


def _route_kernel(lg_ref, idx_ref, gate_ref, rank_ref, kept_ref, *, t, cap):
    bi = pl.program_id(0)
    lg = lg_ref[0]
    ne = lg.shape[0]
    ex = jnp.exp(lg - jnp.max(lg, axis=0, keepdims=True))
    aff = ex / jnp.sum(ex, axis=0, keepdims=True)

    thr = jnp.zeros((ne, 1), jnp.int32)
    for bit in range(30, -1, -1):
        cand = thr | (1 << bit)
        n_ge = jnp.sum(jnp.where(aff >= lax.bitcast_convert_type(cand, F32), 1.0, 0.0), axis=1, keepdims=True)
        thr = jnp.where(n_ge >= cap, cand, thr)
    thr_f = lax.bitcast_convert_type(thr, F32)
    gt = aff > thr_f
    eq = aff == thr_f
    need = cap - jnp.sum(jnp.where(gt, 1.0, 0.0), axis=1, keepdims=True)

    ri = lax.broadcasted_iota(jnp.int32, (LANES, LANES), 0)
    ci = lax.broadcasted_iota(jnp.int32, (LANES, LANES), 1)
    tri = jnp.where(ri < ci, 1.0, 0.0).astype(BF16)
    ones = jnp.ones((LANES, LANES), BF16)
    eq_rank = _lane_excl_cumsum(jnp.where(eq, 1.0, 0.0).astype(BF16), tri, ones)
    sel = gt | (eq & (eq_rank < need))
    rank = _lane_excl_cumsum(jnp.where(sel, 1.0, 0.0).astype(BF16), tri, ones).astype(jnp.int32)
    kept = jnp.where(sel, rank, -1)
    rank_ref[0] = rank
    kept_ref[0] = kept

    tpos = lax.broadcasted_iota(jnp.int32, (1, t), 1)
    t_hi = (tpos >> 6).astype(F32)
    t_lo = (tpos & 63).astype(F32)
    g_hi = aff.astype(BF16).astype(F32)
    g_mid = (aff - g_hi).astype(BF16).astype(F32)
    g_lo = (aff - g_hi) - g_mid
    zrows = jnp.zeros((2 * SUBLANES - 5, t), F32)
    p_iota = lax.broadcasted_iota(jnp.int32, (cap, t), 0)
    for e in range(ne):
        onehot = jnp.where(kept[e:e + 1, :] == p_iota, 1.0, 0.0).astype(BF16)
        vals = jnp.concatenate([t_hi, t_lo, g_hi[e:e + 1], g_mid[e:e + 1], g_lo[e:e + 1], zrows], axis=0).astype(BF16)
        out = _dot_nt(vals, onehot)
        idx_ref[0, e:e + 1, :] = (out[0:1] * 64.0 + out[1:2]).astype(jnp.int32) + bi * t
        gate_ref[0, e:e + 1, :] = (out[2:3] + out[3:4]) + out[4:5]


def _route(lg_t):
    b, ne, t = lg_t.shape
    cap = EC_FACTOR * t // ne
    lst = lambda dt: jax.ShapeDtypeStruct((b, ne, cap), dt)
    dense = jax.ShapeDtypeStruct((b, ne, t), jnp.int32)
    per_b = lambda s: pl.BlockSpec((1,) + s, lambda bi: (bi, 0, 0))
    return pl.pallas_call(
        functools.partial(_route_kernel, t=t, cap=cap),
        grid=(b,),
        in_specs=[per_b((ne, t))],
        out_specs=[per_b((ne, cap)), per_b((ne, cap)), per_b((ne, t)), per_b((ne, t))],
        out_shape=[lst(jnp.int32), lst(F32), dense, dense],
        compiler_params=_params(1),
        name="moe_route",
    )(lg_t)


def _window_tables(rank, cap, tm):
    b, ne, t = rank.shape
    lo = jnp.swapaxes(rank[:, :, ::tm], 1, 2)
    hi = jnp.concatenate([lo[:, 1:], jnp.full((b, 1, ne), cap, jnp.int32)], axis=1)
    span = hi - lo // SUBLANES * SUBLANES
    step = MOE_WINDOW - SUBLANES
    sweeps = jnp.maximum(jnp.max((span + step - 1) // step, axis=2), 1)
    return lo.reshape(-1), sweeps.reshape(-1)


def _moe_ffn_kernel(*refs, n_rt, part_rows):
    n_src = len(part_rows)
    idx_ref, idxn_ref, g_ref = refs[:3]
    h_refs = refs[3:3 + n_src]
    w1_ref, w3_ref, w2_ref = refs[3 + n_src:6 + n_src]
    y_refs = refs[6 + n_src:6 + 2 * n_src]
    stage, xb, acc, sem = refs[6 + 2 * n_src:]
    tr = sum(part_rows)
    f = pl.program_id(2)
    tile = pl.program_id(0) * n_rt + pl.program_id(1)
    n_tiles = pl.num_programs(0) * n_rt
    starts = [sum(part_rows[:i]) for i in range(n_src)]

    def gather(ix_ref):
        for lo, n, h_ref in zip(starts, part_rows, h_refs):
            def issue(p, c, h_ref=h_ref):
                pltpu.make_async_copy(h_ref.at[pl.ds(ix_ref[0, 0, p], 1)], stage.at[pl.ds(p, 1)], sem).start()
                return c
            lax.fori_loop(lo, lo + n, issue, 0, unroll=DMA_ISSUE_UNROLL)

    @pl.when(f == 0)
    def _():
        @pl.when(tile == 0)
        def _():
            gather(idx_ref)
        pltpu.make_async_copy(h_refs[0].at[pl.ds(0, tr)], stage, sem).wait()
        xb[...] = stage[...].astype(BF16)
        acc[...] = jnp.zeros_like(acc)

        @pl.when(tile + 1 < n_tiles)
        def _():
            gather(idxn_ref)

    x = xb[...]
    a1 = _dot(x, w1_ref[0, 0].astype(BF16))
    a3 = _dot(x, w3_ref[0, 0].astype(BF16))
    hmid = (a1 * jax.nn.sigmoid(a1) * a3).astype(BF16)
    acc[...] += _dot(hmid, w2_ref[0, 0].astype(BF16))

    @pl.when(f == pl.num_programs(2) - 1)
    def _():
        for lo, n, y_ref in zip(starts, part_rows, y_refs):
            y_ref[0] = acc[lo:lo + n, :] * g_ref[0, lo:lo + n, :]


def _moe_ffn(idx, gates, hs, w1, w3, w2, layer, part_rows):
    e, n_rt, tr = idx.shape
    d = hs[0].shape[-1]
    ff = w1.shape[-1]
    tf = _row_tile(ff, 256)
    n_src = len(hs)
    tiles = lambda a: a.reshape(e * n_rt, 1, tr)
    smem = lambda imap: pl.BlockSpec((1, 1, tr), imap, memory_space=pltpu.SMEM)
    cur = lambda ei, ri, f: (ei * n_rt + ri, 0, 0)
    nxt = lambda ei, ri, f: (jnp.minimum(ei * n_rt + ri + 1, e * n_rt - 1), 0, 0)
    any_spec = pl.BlockSpec(memory_space=pl.ANY)
    return pl.pallas_call(
        functools.partial(_moe_ffn_kernel, n_rt=n_rt, part_rows=tuple(part_rows)),
        grid=(e, n_rt, ff // tf),
        in_specs=[smem(cur), smem(nxt), pl.BlockSpec((1, tr, 1), lambda ei, ri, f: (ei, ri, 0))]
                 + [any_spec] * n_src
                 + [pl.BlockSpec((1, 1, d, tf), lambda ei, ri, f: (layer, ei, 0, f)),
                    pl.BlockSpec((1, 1, d, tf), lambda ei, ri, f: (layer, ei, 0, f)),
                    pl.BlockSpec((1, 1, tf, d), lambda ei, ri, f: (layer, ei, f, 0))],
        out_specs=[pl.BlockSpec((1, n, d), lambda ei, ri, f: (ei, ri, 0)) for n in part_rows],
        out_shape=[jax.ShapeDtypeStruct((e, n * n_rt, d), F32) for n in part_rows],
        scratch_shapes=[pltpu.VMEM((tr, d), F32), pltpu.VMEM((tr, d), BF16), pltpu.VMEM((tr, d), F32),
                        pltpu.SemaphoreType.DMA],
        compiler_params=_params(3),
        name="moe_ffn",
    )(tiles(idx), tiles(idx), gates.reshape(e, n_rt * tr, 1), *hs, w1, w3, w2)


def _combine_kernel(lo_ref, sw_ref, x_ref, gt_ref, kept_ref, y_ref, fg_ref, o_ref, ybuf, acc, sem, *, cap, final):
    bi = pl.program_id(0)
    step = bi * pl.num_programs(1) + pl.program_id(1)
    n_steps = pl.num_programs(0) * pl.num_programs(1)
    ne = kept_ref.shape[2]
    w = MOE_WINDOW
    al = SUBLANES
    rows_all = y_ref.shape[1]

    def window_start(st, e, c):
        first = (st // pl.num_programs(1)) * cap + lo_ref[st * ne + e] // al * al + c * (w - al)
        return pl.multiple_of(jnp.minimum(first, rows_all - w), al)

    def fetch(st, c, slot, e):
        return pltpu.make_async_copy(y_ref.at[e, pl.ds(window_start(st, e, c), w)],
                                     ybuf.at[slot, pl.ds(e * w, w)], sem.at[slot])

    def start_all(st, c, slot):
        for e in range(ne):
            fetch(st, c, slot, e).start()

    def wait_all(st, c, slot):
        for e in range(ne):
            fetch(st, c, slot, e).wait()

    slot = lax.rem(step, 2)

    @pl.when(step == 0)
    def _():
        start_all(step, 0, slot)

    wait_all(step, 0, slot)

    @pl.when(step + 1 < n_steps)
    def _():
        start_all(step + 1, 0, 1 - slot)

    kept = kept_ref[0]
    lane = lax.broadcasted_iota(jnp.int32, (1, 2 * w), 1)
    first_half = lane < w
    row_in = jnp.where(first_half, lane, lane - w)
    acc[...] = jnp.zeros_like(acc)

    def add_windows(c, sl):
        span = jnp.where(c == sw_ref[step] - 1, rows_all, w - al)
        pieces = []
        for e2 in range(0, ne, 2):
            pos = jnp.where(first_half, kept[:, e2:e2 + 1], kept[:, e2 + 1:e2 + 2])
            off = jnp.where(first_half, window_start(step, e2, c), window_start(step, e2 + 1, c)) - bi * cap
            own_lo = jnp.where(first_half, lo_ref[step * ne + e2] // al * al,
                               lo_ref[step * ne + e2 + 1] // al * al) + c * (w - al)
            hit = (pos - off == row_in) & (pos >= own_lo) & (pos < own_lo + span)
            pieces.append(jnp.where(hit, 1.0, 0.0).astype(BF16))
        own = jnp.concatenate(pieces, axis=1)
        acc[...] += _dot(own, ybuf[sl].astype(BF16))

    add_windows(0, slot)

    def extra(c, carry):
        start_all(step, c, slot)
        wait_all(step, c, slot)
        add_windows(c, slot)
        return carry

    lax.fori_loop(1, sw_ref[step], extra, 0)
    xn = x_ref[0] + gt_ref[0] * acc[...]
    o_ref[0] = _rms_scale(xn) * fg_ref[...] if final else xn


def _combine(x, gt, kept_t, y, tables, final_g, final):
    b, t, d = x.shape
    ne = kept_t.shape[2]
    cap = EC_FACTOR * t // ne
    tm = _row_tile(t, MOE_TOKEN_TILE)
    lo, sweeps = tables
    tok = lambda wd: pl.BlockSpec((1, tm, wd), lambda bi, i, *_: (bi, i, 0))
    return pl.pallas_call(
        functools.partial(_combine_kernel, cap=cap, final=final),
        grid_spec=pltpu.PrefetchScalarGridSpec(
            num_scalar_prefetch=2,
            grid=(b, t // tm),
            in_specs=[tok(d), pl.BlockSpec((1, 1, d), lambda bi, i, *_: (bi, 0, 0)), tok(ne),
                      pl.BlockSpec(memory_space=pl.ANY), pl.BlockSpec(final_g.shape, lambda bi, i, *_: (0, 0))],
            out_specs=tok(d),
            scratch_shapes=[pltpu.VMEM((2, ne * MOE_WINDOW, d), F32), pltpu.VMEM((tm, d), F32),
                            pltpu.SemaphoreType.DMA((2,))]),
        out_shape=jax.ShapeDtypeStruct((b, t, d), F32),
        compiler_params=_params(2),
        name="moe_combine",
    )(lo, sweeps, x, gt, kept_t, y, final_g)


def _rope_tables(t):
    rows = t // GRID_W
    row = jnp.repeat(jnp.arange(rows, dtype=F32), GRID_W)
    col = jnp.tile(jnp.arange(GRID_W, dtype=F32), rows)
    inv = ROPE_BASE ** (-jnp.arange(0, ROPE_AXIS, 2, dtype=F32) / ROPE_AXIS)
    ar = row[:, None] * inv
    ac = col[:, None] * inv
    cos = jnp.concatenate([jnp.cos(ar), jnp.cos(ar), jnp.cos(ac), jnp.cos(ac)], axis=-1)
    sin = jnp.concatenate([jnp.sin(ar), jnp.sin(ar), jnp.sin(ac), jnp.sin(ac)], axis=-1)
    return jnp.concatenate([cos, sin], axis=-1)


def _rot_columns(w):
    half = ROPE_AXIS // 2
    return jnp.concatenate([-w[..., half:ROPE_AXIS], w[..., :half],
                            -w[..., ROPE_AXIS + half:], w[..., ROPE_AXIS:ROPE_AXIS + half]], axis=-1)


def _moe(streams, w1, w3, w2, layer):
    ne = w1.shape[1]
    n_rt = MOE_ROW_TILES
    idxs, gates, hs, infos = [], [], [], []
    for h, lg_t in streams:
        b, t, d = h.shape
        idx, gate, rank, kept = _route(lg_t)
        by_tile = lambda a: jnp.swapaxes(a, 0, 1).reshape(ne, n_rt, -1)
        idxs.append(by_tile(idx))
        gates.append(by_tile(gate))
        hs.append(h.reshape(b * t, d))
        infos.append((jnp.swapaxes(kept, 1, 2), _window_tables(rank, EC_FACTOR * t // ne, _row_tile(t, MOE_TOKEN_TILE))))
    ys = _moe_ffn(jnp.concatenate(idxs, axis=2), jnp.concatenate(gates, axis=2), hs, w1, w3, w2, layer,
                  [a.shape[2] for a in idxs])
    return [(y,) + info for y, info in zip(ys, infos)]


def kernel(x, c, ctx, c_ctx, w_ada, b_ada, norm_g, final_g, w_in_e, g_qa, w_uq, g_kv, w_ukv, w_pool, pool_scale,
           w_out_e, w_in_o, conv_w, conv_b, w_rg_a, b_rg_a, w_rg_x, b_rg_x, lru_lambda, w_out_o, w_router, w_e1,
           w_e3, w_e2):
    b, t, d = x.shape
    tc = ctx.shape[1]
    depth = w_ada.shape[0]
    n_heads = d // (2 * V_HEAD)
    pool_w = d // 2

    c_rows = jnp.concatenate([c, c_ctx[None], jnp.zeros((SUBLANES - b - 1, d), F32)], axis=0)
    mods = _mods(c_rows, w_ada, b_ada)

    def mod_rows(layer, k):
        m = mods[layer, :, k * d:(k + 1) * d]
        return m[:b, None, :], jnp.broadcast_to(m[b][None, None, :], (b, 1, d))

    cs_lat = _rope_tables(t)
    cs_ctx = jnp.concatenate([jnp.ones((tc, QK_ROPE), F32), jnp.zeros((tc, QK_ROPE), F32)], axis=-1)

    cs = ctx
    for layer in range(depth):
        last = layer == depth - 1
        (sh_m, csh_m), (sc_m, csc_m), (gt_m, cgt_m), (sh_f, csh_f), (sc_f, csc_f), (gt_f, cgt_f) = (
            mod_rows(layer, k) for k in range(N_MOD))
        g1 = norm_g[layer, 0][None]
        g2 = norm_g[layer, 1][None]
        if layer % 2 == 0:
            e = layer // 2
            wi = w_in_e[e]
            o_kpe = Q_LORA + KV_LORA
            kpe = wi[:, o_kpe:o_kpe + QK_ROPE]
            win = jnp.concatenate([wi[:, :o_kpe], wi[:, o_kpe + QK_ROPE:], kpe, _rot_columns(kpe)], axis=1).astype(BF16)
            wq3 = w_uq[e].reshape(Q_LORA, n_heads, QK_HEAD)
            wq = jnp.concatenate([wq3, _rot_columns(wq3[..., QK_NOPE:])], axis=-1).reshape(Q_LORA, n_heads * HEAD_PAD)
            wkv3 = w_ukv[e].reshape(KV_LORA, n_heads, QK_NOPE + V_HEAD)
            wk = wkv3[..., :QK_NOPE].reshape(KV_LORA, n_heads * QK_NOPE).astype(BF16)
            wv = wkv3[..., QK_NOPE:].reshape(KV_LORA, n_heads * V_HEAD).astype(BF16)
            pre = functools.partial(_even_pre, g=g1, win=win, gqa=g_qa[e][None], gkv=g_kv[e][None],
                                    wq=wq.astype(BF16), wk=wk, wv=wv, n_heads=n_heads)
            q_l, k_l, v_l, u_l = pre(x, sc_m, sh_m, cs_tab=cs_lat)
            q_c, k_c, v_c, u_c = pre(cs, csc_m, csh_m, cs_tab=cs_ctx)
            wo = w_out_e[e].astype(BF16)
            wo_a, wo_p = wo[:n_heads * V_HEAD], wo[n_heads * V_HEAD:]
            a_l = _attention(q_l, [(k_c, v_c), (k_l, v_l)], n_heads)
            pl_l = _pool(u_l, w_pool[e].astype(BF16), pool_scale[e][None])
            x, h_l, lg_l = _mix_out([a_l, pl_l], [wo_a, wo_p], x, gt_m, g2, sc_f, sh_f, w_router[layer].T)
            streams = [(h_l, lg_l)]
            if not last:
                a_c = _attention(q_c, [(k_c, v_c)], n_heads)
                pl_c = _pool(u_c, w_pool[e].astype(BF16), pool_scale[e][None])
                cs, h_c, lg_c = _mix_out([a_c, pl_c], [wo_a, wo_p], cs, cgt_m, g2, csc_f, csh_f, w_router[layer].T)
                streams.append((h_c, lg_c))
        else:
            o = layer // 2
            wi = w_in_o[o].astype(BF16)
            lw = wi.shape[1] // 2
            p_l = _norm_matmul(x, sc_m, sh_m, g1, wi)
            p_c = _norm_matmul(cs, csc_m, csh_m, g1, wi if not last else wi[:, lw:])
            wo = w_out_o[o].astype(BF16)
            y_l = _rglru(p_l, p_c[..., p_c.shape[-1] - lw:], conv_w[o], conv_b[o], w_rg_a[o].astype(BF16), b_rg_a[o],
                         w_rg_x[o].astype(BF16), b_rg_x[o], lru_lambda[o])
            x, h_l, lg_l = _mix_out([y_l], [wo], x, gt_m, g2, sc_f, sh_f, w_router[layer].T)
            streams = [(h_l, lg_l)]
            if not last:
                raise NotImplementedError("context output of an RG-LRU layer is only needed when it is not the last layer")
        outs = _moe(streams, w_e1, w_e3, w_e2, layer)
        y_l, kept_l, tab_l = outs[0]
        x = _combine(x, gt_f, kept_l, y_l, tab_l, final_g[None], final=last)
        if not last:
            y_c, kept_c, tab_c = outs[1]
            cs = _combine(cs, cgt_f, kept_c, y_c, tab_c, final_g[None], final=False)
    return x
```

```python
import functools
import math

import jax
import jax.numpy as jnp
from jax import lax
from jax.experimental import pallas as pl
from jax.experimental.pallas import tpu as pltpu

F32 = jnp.float32
BF16 = jnp.bfloat16

RMS_EPS = 1e-6
N_MOD = 6
GRID_W = 64
ROPE_BASE = 10000.0

V_HEAD = 128
QK_NOPE = 128
QK_ROPE = 64
QK_HEAD = QK_NOPE + QK_ROPE
Q_LORA = 512
KV_LORA = 512
ROPE_AXIS = QK_ROPE // 2
HEAD_PAD = 256
KEY_CHUNK = 256

POOL_WINDOWS = (2, 4, 8, 16)
POOL_PAD = 16

LRU_BLOCKS = 8
CONV_W = 4
CONV_PAD = 8
LRU_C = 8.0

N_EXPERTS = 16
EC_FACTOR = 2
MOE_ROW_TILES = 2
MOE_TOKEN_TILE = 256
MOE_WINDOW = 64

DMA_ISSUE_UNROLL = 8
SCAN_UNROLL = 8

SUBLANES = 8
LANES = 128
VMEM_LIMIT_BYTES = 56 * 1024 * 1024


def _params(n_grid):
    return pltpu.CompilerParams(dimension_semantics=("arbitrary",) * n_grid,
                                vmem_limit_bytes=VMEM_LIMIT_BYTES)


def _const_spec(shape):
    nd = len(shape)
    return pl.BlockSpec(shape, lambda *_: (0,) * nd, pipeline_mode=pl.Buffered(1))


def _rms_scale(xf):
    return xf * lax.rsqrt(jnp.mean(xf * xf, axis=-1, keepdims=True) + RMS_EPS)


def _dot(a, b):
    return jnp.dot(a, b, preferred_element_type=F32)


def _dot_nt(a, b):
    return lax.dot_general(a, b, (((1,), (1,)), ((), ())), preferred_element_type=F32)


def _row_tile(t, pref):
    return pref if t % pref == 0 else t


def _mods_kernel(c_ref, w_ref, b_ref, o_ref):
    c = c_ref[...]
    s = c * jax.nn.sigmoid(c)
    o_ref[0] = _dot(s.astype(BF16), w_ref[0].astype(BF16)) + b_ref[0]


def _mods(c_rows, w_ada, b_ada):
    depth, d, n = w_ada.shape
    tn = 1024
    return pl.pallas_call(
        _mods_kernel,
        grid=(depth, n // tn),
        in_specs=[_const_spec(c_rows.shape),
                  pl.BlockSpec((1, d, tn), lambda l, j: (l, 0, j)),
                  pl.BlockSpec((1, 1, tn), lambda l, j: (l, 0, j))],
        out_specs=pl.BlockSpec((1, c_rows.shape[0], tn), lambda l, j: (l, 0, j)),
        out_shape=jax.ShapeDtypeStruct((depth, c_rows.shape[0], n), F32),
        compiler_params=_params(2),
        name="adaln_mods",
    )(c_rows, w_ada, b_ada.reshape(depth, 1, n))


def _even_pre_kernel(x_ref, sc_ref, sh_ref, g_ref, win_ref, gqa_ref, gkv_ref, wq_ref, wk_ref, wv_ref, cs_ref,
                     q_ref, k_ref, v_ref, u_ref, *, n_heads, scale, pool_w):
    h = _rms_scale(x_ref[0]) * (g_ref[...] * (1.0 + sc_ref[0])) + sh_ref[0]
    p = _dot(h.astype(BF16), win_ref[...])
    cs = cs_ref[...]
    o_pool = Q_LORA + KV_LORA
    o_rope = o_pool + pool_w

    ql = _rms_scale(p[:, :Q_LORA]) * gqa_ref[...]
    q = _dot(ql.astype(BF16), wq_ref[...])
    for hd in range(n_heads):
        base = hd * HEAD_PAD
        hi = q[:, base + QK_NOPE:base + HEAD_PAD] * cs
        hi = hi + pltpu.roll(hi, QK_ROPE, axis=1)
        q_ref[0, :, base:base + QK_NOPE] = (q[:, base:base + QK_NOPE] * scale).astype(BF16)
        q_ref[0, :, base + QK_NOPE:base + HEAD_PAD] = (hi * scale).astype(BF16)

    kvl = (_rms_scale(p[:, Q_LORA:o_pool]) * gkv_ref[...]).astype(BF16)
    kn = _dot(kvl, wk_ref[...])
    v_ref[0] = _dot(kvl, wv_ref[...]).astype(BF16)
    kp = p[:, o_rope:o_rope + 2 * QK_ROPE] * cs
    kp = kp + pltpu.roll(kp, QK_ROPE, axis=1)
    lane = lax.broadcasted_iota(jnp.int32, kp.shape, 1)
    kp = jnp.where(lane < QK_ROPE, kp, 0.0).astype(BF16)
    for hd in range(n_heads):
        base = hd * HEAD_PAD
        k_ref[0, :, base:base + QK_NOPE] = kn[:, hd * QK_NOPE:(hd + 1) * QK_NOPE].astype(BF16)
        k_ref[0, :, base + QK_NOPE:base + HEAD_PAD] = kp
    u_ref[0] = p[:, o_pool:o_rope]


def _even_pre(x, sc, sh, g, win, gqa, gkv, wq, wk, wv, cs_tab, n_heads):
    b, t, d = x.shape
    tm = _row_tile(t, 256)
    pool_w = d // 2
    kern = functools.partial(_even_pre_kernel, n_heads=n_heads, scale=QK_HEAD ** -0.5 * math.log2(math.e),
                             pool_w=pool_w)
    tok = lambda w: pl.BlockSpec((1, tm, w), lambda bi, i: (bi, i, 0))
    per_b = pl.BlockSpec((1, 1, d), lambda bi, i: (bi, 0, 0))
    return pl.pallas_call(
        kern,
        grid=(b, t // tm),
        in_specs=[tok(d), per_b, per_b, _const_spec(g.shape), _const_spec(win.shape), _const_spec(gqa.shape),
                  _const_spec(gkv.shape), _const_spec(wq.shape), _const_spec(wk.shape), _const_spec(wv.shape),
                  pl.BlockSpec((tm, 2 * QK_ROPE), lambda bi, i: (i, 0))],
        out_specs=[tok(n_heads * HEAD_PAD), tok(n_heads * HEAD_PAD), tok(n_heads * V_HEAD), tok(pool_w)],
        out_shape=[jax.ShapeDtypeStruct((b, t, n_heads * HEAD_PAD), BF16),
                   jax.ShapeDtypeStruct((b, t, n_heads * HEAD_PAD), BF16),
                   jax.ShapeDtypeStruct((b, t, n_heads * V_HEAD), BF16),
                   jax.ShapeDtypeStruct((b, t, pool_w), F32)],
        compiler_params=_params(2),
        name="even_pre",
    )(x, sc, sh, g, win, gqa, gkv, wq, wk, wv, cs_tab)


def _attn_kernel(*refs, chunks):
    q_ref, o_ref = refs[0], refs[-1]
    q = q_ref[0]
    m = l = acc = None
    for i, s0, n in chunks:
        s = _dot_nt(q, refs[1 + 2 * i][0, s0:s0 + n, :])
        v = refs[2 + 2 * i][0, s0:s0 + n, :]
        mc = jnp.max(s, axis=-1, keepdims=True)
        if m is None:
            m_new = mc
            p = jnp.exp2(s - m_new)
            l = jnp.sum(p, axis=-1, keepdims=True)
            acc = _dot(p.astype(BF16), v)
        else:
            m_new = jnp.maximum(m, mc)
            alpha = jnp.exp2(m - m_new)
            p = jnp.exp2(s - m_new)
            l = alpha * l + jnp.sum(p, axis=-1, keepdims=True)
            acc = alpha * acc + _dot(p.astype(BF16), v)
        m = m_new
    o_ref[0] = (acc * (1.0 / l)).astype(BF16)


def _attention(q, kvs, n_heads):
    b, tq_all, _ = q.shape
    tq = _row_tile(tq_all, 2048)
    in_specs = [pl.BlockSpec((1, tq, HEAD_PAD), lambda bi, hd, i: (bi, i, hd))]
    args = [q]
    chunks = []
    for j, (k, v) in enumerate(kvs):
        s = k.shape[1]
        tk = _row_tile(s, KEY_CHUNK)
        chunks += [(j, s0, tk) for s0 in range(0, s, tk)]
        in_specs.append(pl.BlockSpec((1, s, HEAD_PAD), lambda bi, hd, i: (bi, 0, hd)))
        in_specs.append(pl.BlockSpec((1, s, V_HEAD), lambda bi, hd, i: (bi, 0, hd)))
        args += [k, v]
    return pl.pallas_call(
        functools.partial(_attn_kernel, chunks=tuple(chunks)),
        grid=(b, n_heads, tq_all // tq),
        in_specs=in_specs,
        out_specs=pl.BlockSpec((1, tq, V_HEAD), lambda bi, hd, i: (bi, i, hd)),
        out_shape=jax.ShapeDtypeStruct((b, tq_all, n_heads * V_HEAD), BF16),
        compiler_params=_params(3),
        name="mla_attention",
    )(*args)


def _pool_kernel(u_ref, w_ref, ps_ref, o_ref, e0, e1, *, t, windows):
    gi = pl.program_id(1)
    gc = u_ref.shape[-1]
    u = u_ref[0]
    zpad = jnp.zeros((POOL_PAD, gc), F32)
    for e in (e0, e1):
        e[0:POOL_PAD, :] = zpad
        e[POOL_PAD + t:2 * POOL_PAD + t, :] = zpad
    e0[POOL_PAD:POOL_PAD + t, :] = u
    pos = lax.broadcasted_iota(jnp.int32, (t, 1), 0)
    n = t + 2 * POOL_PAD - 2 * SUBLANES
    for k, w in enumerate(windows):
        @pl.when(gi == k)
        def _(w=w):
            src, dst = e0, e1
            dst[SUBLANES:SUBLANES + n, :] = src[SUBLANES - 1:SUBLANES - 1 + n, :] + src[SUBLANES:SUBLANES + n, :]
            src, dst = dst, src
            m = 2
            while m < w:
                hm = m // 2
                dst[SUBLANES:SUBLANES + n, :] = (src[SUBLANES - hm:SUBLANES - hm + n, :]
                                                 + src[SUBLANES + hm:SUBLANES + hm + n, :])
                src, dst = dst, src
                m *= 2
            tot = src[POOL_PAD:POOL_PAD + t, :]
            cnt = (jnp.minimum(pos + (w - w // 2), t) - jnp.maximum(pos - w // 2, 0)).astype(F32)
            pooled = tot / cnt - u
            o_ref[0] = (_dot(pooled.astype(BF16), w_ref[0]) * ps_ref[...]).astype(BF16)


def _pool(u, w_pool, pool_scale):
    b, t, pw = u.shape
    ng = len(POOL_WINDOWS)
    gc = pw // ng
    return pl.pallas_call(
        functools.partial(_pool_kernel, t=t, windows=POOL_WINDOWS),
        grid=(b, ng),
        in_specs=[pl.BlockSpec((1, t, gc), lambda bi, g: (bi, 0, g)),
                  pl.BlockSpec((1, gc, gc), lambda bi, g: (g, 0, 0)),
                  pl.BlockSpec((1, gc), lambda bi, g: (0, g))],
        out_specs=pl.BlockSpec((1, t, gc), lambda bi, g: (bi, 0, g)),
        out_shape=jax.ShapeDtypeStruct((b, t, pw), BF16),
        scratch_shapes=[pltpu.VMEM((t + 2 * POOL_PAD, gc), F32), pltpu.VMEM((t + 2 * POOL_PAD, gc), F32)],
        compiler_params=_params(2),
        name="pool_mixer",
    )(u, w_pool, pool_scale)


def _mix_out_kernel(*refs, n_a):
    a_refs, w_refs = refs[:n_a], refs[n_a:2 * n_a]
    x_ref, gt_ref, g2_ref, sc_ref, sh_ref, wr_ref, xo_ref, h_ref, lg_ref = refs[2 * n_a:]
    m = functools.reduce(jnp.add, [_dot(a_refs[i][0], w_refs[i][...]) for i in range(n_a)])
    xn = x_ref[0] + gt_ref[0] * m
    xo_ref[0] = xn
    hf = _rms_scale(xn) * (g2_ref[...] * (1.0 + sc_ref[0])) + sh_ref[0]
    h_ref[0] = hf
    hb = hf.astype(BF16)
    h_lo = (hf - hb.astype(F32)).astype(BF16)
    wr = wr_ref[...]
    w_hi = wr.astype(BF16)
    w_lo = (wr - w_hi.astype(F32)).astype(BF16)
    lg_ref[0] = _dot_nt(w_hi, hb) + (_dot_nt(w_lo, hb) + _dot_nt(w_hi, h_lo))


def _mix_out(a_list, w_list, x, gt, g2, sc, sh, w_router_t):
    b, t, d = x.shape
    tm = _row_tile(t, 512)
    ne = w_router_t.shape[0]
    tok = lambda w: pl.BlockSpec((1, tm, w), lambda bi, i: (bi, i, 0))
    per_b = pl.BlockSpec((1, 1, d), lambda bi, i: (bi, 0, 0))
    in_specs = ([tok(a.shape[-1]) for a in a_list] + [_const_spec(w.shape) for w in w_list]
                + [tok(d), per_b, _const_spec(g2.shape), per_b, per_b, _const_spec(w_router_t.shape)])
    return pl.pallas_call(
        functools.partial(_mix_out_kernel, n_a=len(a_list)),
        grid=(b, t // tm),
        in_specs=in_specs,
        out_specs=[tok(d), tok(d), pl.BlockSpec((1, ne, tm), lambda bi, i: (bi, 0, i))],
        out_shape=[jax.ShapeDtypeStruct((b, t, d), F32), jax.ShapeDtypeStruct((b, t, d), F32),
                   jax.ShapeDtypeStruct((b, ne, t), F32)],
        compiler_params=_params(2),
        name="mix_out",
    )(*a_list, *w_list, x, gt, g2, sc, sh, w_router_t)


def _norm_matmul_kernel(x_ref, sc_ref, sh_ref, g_ref, w_ref, o_ref):
    h = _rms_scale(x_ref[0]) * (g_ref[...] * (1.0 + sc_ref[0])) + sh_ref[0]
    o_ref[0] = _dot(h.astype(BF16), w_ref[...])


def _norm_matmul(x, sc, sh, g, w):
    b, t, d = x.shape
    n = w.shape[1]
    tm = _row_tile(t, 512)
    tn = _row_tile(n, 2048)
    return pl.pallas_call(
        _norm_matmul_kernel,
        grid=(n // tn, b, t // tm),
        in_specs=[pl.BlockSpec((1, tm, d), lambda j, bi, i: (bi, i, 0)),
                  pl.BlockSpec((1, 1, d), lambda j, bi, i: (bi, 0, 0)),
                  pl.BlockSpec((1, 1, d), lambda j, bi, i: (bi, 0, 0)),
                  _const_spec(g.shape),
                  pl.BlockSpec((d, tn), lambda j, bi, i: (0, j))],
        out_specs=pl.BlockSpec((1, tm, tn), lambda j, bi, i: (bi, i, j)),
        out_shape=jax.ShapeDtypeStruct((b, t, n), F32),
        compiler_params=_params(3),
        name="norm_matmul",
    )(x, sc, sh, g, w)


def _rglru_kernel(gate_ref, rec_ref, recc_ref, cw_ref, cb_ref, wa_ref, ba_ref, wx_ref, bx_ref, lam_ref, o_ref,
                  ext, extc, ubuf, ubufc, abuf, bbuf, ybuf, *, t, tc, chunk):
    c = rec_ref.shape[-1]
    zpad = jnp.zeros((CONV_PAD, c), F32)
    for e, r, n in ((ext, rec_ref, t), (extc, recc_ref, tc)):
        e[0:CONV_PAD, :] = zpad
        e[CONV_PAD + n:2 * CONV_PAD + n, :] = zpad
        e[CONV_PAD:CONV_PAD + n, :] = r[0]
    cw = cw_ref[...]
    cb = cb_ref[...]
    row = lax.broadcasted_iota(jnp.int32, (SUBLANES, c), 0)

    def gates(e_ref, u_ref, r0, n, d):
        if d == 0:
            u = cb
            for k in range(CONV_W):
                off = CONV_PAD + r0 + k - CONV_W // 2
                u = u + cw[k:k + 1] * e_ref[off:off + n, :]
            u_ref[r0:r0 + n, :] = u
        else:
            u = u_ref[r0:r0 + n, :]
        ub = u.astype(BF16)
        r = jax.nn.sigmoid(_dot(ub, wa_ref[d, 0]) + ba_ref[d:d + 1])
        i = jax.nn.sigmoid(_dot(ub, wx_ref[d, 0]) + bx_ref[d:d + 1])
        log_a = (-LRU_C * r) * jax.nn.softplus(-lam_ref[d:d + 1])
        a = jnp.exp(log_a)
        abuf[0:n, :] = a
        bbuf[0:n, :] = jnp.sqrt(1.0 - a * a) * (i * u)

    def scan(n, carry, d, emit):
        nb = n // SUBLANES

        def body(j, hc):
            jj = j if d == 0 else nb - 1 - j
            off = pl.multiple_of(jj * SUBLANES, SUBLANES)
            a = abuf[pl.ds(off, SUBLANES), :]
            bb = bbuf[pl.ds(off, SUBLANES), :]
            for s in (1, 2, 4):
                if d == 0:
                    sh, msk = s, row >= s
                else:
                    sh, msk = SUBLANES - s, row < SUBLANES - s
                a_s = jnp.where(msk, pltpu.roll(a, sh, axis=0), 1.0)
                b_s = jnp.where(msk, pltpu.roll(bb, sh, axis=0), 0.0)
                bb = a * b_s + bb
                a = a * a_s
            h = a * hc + bb
            emit(off, h)
            return h[SUBLANES - 1:SUBLANES] if d == 0 else h[0:1]

        return lax.fori_loop(0, nb, body, carry, unroll=SCAN_UNROLL)

    def run(e_ref, u_ref, n_all, d, carry, emit):
        cn = min(chunk, n_all)
        starts = list(range(0, n_all, cn))
        for r0 in (starts if d == 0 else starts[::-1]):
            gates(e_ref, u_ref, r0, cn, d)
            carry = scan(cn, carry, d, functools.partial(emit, r0))
        return carry

    def emit_none(r0, off, h):
        pass

    def emit_set(r0, off, h):
        ybuf[pl.ds(r0 + off, SUBLANES), :] = h

    def emit_add(r0, off, h):
        ybuf[pl.ds(r0 + off, SUBLANES), :] += h

    h0 = jnp.zeros((1, c), F32)
    run(ext, ubuf, t, 0, run(extc, ubufc, tc, 0, h0, emit_none), emit_set)
    run(ext, ubuf, t, 1, run(extc, ubufc, tc, 1, h0, emit_none), emit_add)
    cn = min(chunk, t)
    for r0 in range(0, t, cn):
        o_ref[0, r0:r0 + cn, :] = (ybuf[r0:r0 + cn, :] * jax.nn.gelu(gate_ref[0, r0:r0 + cn, :])).astype(BF16)


def _rglru(p_lat, p_ctx, conv_w, conv_b, w_a, b_a, w_x, b_x, lam):
    b, t, c2 = p_lat.shape
    cw = c2 // 2
    tc = p_ctx.shape[1]
    nb = w_a.shape[1]
    bs = cw // nb
    chunk = 512
    vec = lambda rows: pl.BlockSpec((rows, bs), lambda bi, n: (0, n))
    wspec = pl.BlockSpec((2, 1, bs, bs), lambda bi, n: (0, n, 0, 0))
    return pl.pallas_call(
        functools.partial(_rglru_kernel, t=t, tc=tc, chunk=chunk),
        grid=(b, nb),
        in_specs=[pl.BlockSpec((1, t, bs), lambda bi, n: (bi, 0, n)),
                  pl.BlockSpec((1, t, bs), lambda bi, n: (bi, 0, nb + n)),
                  pl.BlockSpec((1, tc, bs), lambda bi, n: (bi, 0, n)),
                  vec(CONV_W), vec(1), wspec, vec(2), wspec, vec(2), vec(2)],
        out_specs=pl.BlockSpec((1, t, bs), lambda bi, n: (bi, 0, n)),
        out_shape=jax.ShapeDtypeStruct((b, t, cw), BF16),
        scratch_shapes=[pltpu.VMEM((t + 2 * CONV_PAD, bs), F32), pltpu.VMEM((tc + 2 * CONV_PAD, bs), F32),
                        pltpu.VMEM((t, bs), F32), pltpu.VMEM((tc, bs), F32),
                        pltpu.VMEM((min(chunk, max(t, tc)), bs), F32), pltpu.VMEM((min(chunk, max(t, tc)), bs), F32),
                        pltpu.VMEM((t, bs), F32)],
        compiler_params=_params(2),
        name="rglru",
    )(p_lat, p_lat, p_ctx, conv_w, conv_b.reshape(1, cw), w_a, b_a, w_x, b_x, lam)


def _lane_excl_cumsum(x, tri, ones):
    r, t = x.shape
    run = jnp.zeros((r, LANES), F32)
    outs = []
    for k in range(t // LANES):
        blk = x[:, k * LANES:(k + 1) * LANES]
        outs.append(_dot(blk, tri) + run)
        run = run + _dot(blk, ones)
    return jnp.concatenate(outs, axis=1)


def _route_kernel(lg_ref, idx_ref, gate_ref, rank_ref, kept_ref, *, t, cap):
    bi = pl.program_id(0)
    lg = lg_ref[0]
    ne = lg.shape[0]
    ex = jnp.exp(lg - jnp.max(lg, axis=0, keepdims=True))
    aff = ex / jnp.sum(ex, axis=0, keepdims=True)

    thr = jnp.zeros((ne, 1), jnp.int32)
    for bit in range(30, -1, -1):
        cand = thr | (1 << bit)
        n_ge = jnp.sum(jnp.where(aff >= lax.bitcast_convert_type(cand, F32), 1.0, 0.0), axis=1, keepdims=True)
        thr = jnp.where(n_ge >= cap, cand, thr)
    thr_f = lax.bitcast_convert_type(thr, F32)
    gt = aff > thr_f
    eq = aff == thr_f
    need = cap - jnp.sum(jnp.where(gt, 1.0, 0.0), axis=1, keepdims=True)

    ri = lax.broadcasted_iota(jnp.int32, (LANES, LANES), 0)
    ci = lax.broadcasted_iota(jnp.int32, (LANES, LANES), 1)
    tri = jnp.where(ri < ci, 1.0, 0.0).astype(BF16)
    ones = jnp.ones((LANES, LANES), BF16)
    eq_rank = _lane_excl_cumsum(jnp.where(eq, 1.0, 0.0).astype(BF16), tri, ones)
    sel = gt | (eq & (eq_rank < need))
    rank = _lane_excl_cumsum(jnp.where(sel, 1.0, 0.0).astype(BF16), tri, ones).astype(jnp.int32)
    kept = jnp.where(sel, rank, -1)
    rank_ref[0] = rank
    kept_ref[0] = kept

    tpos = lax.broadcasted_iota(jnp.int32, (1, t), 1)
    t_hi = (tpos >> 6).astype(F32)
    t_lo = (tpos & 63).astype(F32)
    g_hi = aff.astype(BF16).astype(F32)
    g_mid = (aff - g_hi).astype(BF16).astype(F32)
    g_lo = (aff - g_hi) - g_mid
    zrows = jnp.zeros((2 * SUBLANES - 5, t), F32)
    p_iota = lax.broadcasted_iota(jnp.int32, (cap, t), 0)
    for e in range(ne):
        onehot = jnp.where(kept[e:e + 1, :] == p_iota, 1.0, 0.0).astype(BF16)
        vals = jnp.concatenate([t_hi, t_lo, g_hi[e:e + 1], g_mid[e:e + 1], g_lo[e:e + 1], zrows], axis=0).astype(BF16)
        out = _dot_nt(vals, onehot)
        idx_ref[0, e:e + 1, :] = (out[0:1] * 64.0 + out[1:2]).astype(jnp.int32) + bi * t
        gate_ref[0, e:e + 1, :] = (out[2:3] + out[3:4]) + out[4:5]


def _route(lg_t):
    b, ne, t = lg_t.shape
    cap = EC_FACTOR * t // ne
    lst = lambda dt: jax.ShapeDtypeStruct((b, ne, cap), dt)
    dense = jax.ShapeDtypeStruct((b, ne, t), jnp.int32)
    per_b = lambda s: pl.BlockSpec((1,) + s, lambda bi: (bi, 0, 0))
    return pl.pallas_call(
        functools.partial(_route_kernel, t=t, cap=cap),
        grid=(b,),
        in_specs=[per_b((ne, t))],
        out_specs=[per_b((ne, cap)), per_b((ne, cap)), per_b((ne, t)), per_b((ne, t))],
        out_shape=[lst(jnp.int32), lst(F32), dense, dense],
        compiler_params=_params(1),
        name="moe_route",
    )(lg_t)


def _window_tables(rank, cap, tm):
    b, ne, t = rank.shape
    lo = jnp.swapaxes(rank[:, :, ::tm], 1, 2)
    hi = jnp.concatenate([lo[:, 1:], jnp.full((b, 1, ne), cap, jnp.int32)], axis=1)
    span = hi - lo // SUBLANES * SUBLANES
    step = MOE_WINDOW - SUBLANES
    sweeps = jnp.maximum(jnp.max((span + step - 1) // step, axis=2), 1)
    return lo.reshape(-1), sweeps.reshape(-1)


def _moe_ffn_kernel(*refs, n_rt, part_rows, nf):
    n_src = len(part_rows)
    idx_ref, idxn_ref, g_ref = refs[:3]
    h_refs = refs[3:3 + n_src]
    w1_ref, w3_ref, w2_ref = refs[3 + n_src:6 + n_src]
    y_refs = refs[6 + n_src:6 + 2 * n_src]
    stage, xb, hmid, sem = refs[6 + 2 * n_src:]
    tr = sum(part_rows)
    tf = w1_ref.shape[-1]
    s = pl.program_id(2)
    tile = pl.program_id(0) * n_rt + pl.program_id(1)
    n_tiles = pl.num_programs(0) * n_rt
    starts = [sum(part_rows[:i]) for i in range(n_src)]

    def gather(ix_ref):
        for lo, n, h_ref in zip(starts, part_rows, h_refs):
            def issue(p, c, h_ref=h_ref):
                pltpu.make_async_copy(h_ref.at[pl.ds(ix_ref[0, 0, p], 1)], stage.at[pl.ds(p, 1)], sem).start()
                return c
            lax.fori_loop(lo, lo + n, issue, 0, unroll=DMA_ISSUE_UNROLL)

    @pl.when(s == 0)
    def _():
        @pl.when(tile == 0)
        def _():
            gather(idx_ref)
        pltpu.make_async_copy(h_refs[0].at[pl.ds(0, tr)], stage, sem).wait()
        xb[...] = stage[...].astype(BF16)

        @pl.when(tile + 1 < n_tiles)
        def _():
            gather(idxn_ref)

    @pl.when(s < nf)
    def _():
        x = xb[...]
        a1 = _dot(x, w1_ref[0, 0].astype(BF16))
        a3 = _dot(x, w3_ref[0, 0].astype(BF16))
        hm = (a1 * jax.nn.sigmoid(a1) * a3).astype(BF16)
        for f in range(nf):
            @pl.when(s == f)
            def _(f=f):
                hmid[:, f * tf:(f + 1) * tf] = hm

    @pl.when(s >= nf)
    def _():
        y = _dot(hmid[...], w2_ref[0, 0].astype(BF16))
        for lo, n, y_ref in zip(starts, part_rows, y_refs):
            y_ref[0] = y[lo:lo + n, :] * g_ref[0, lo:lo + n, :]


def _moe_ffn(idx, gates, hs, w1, w3, w2, layer, part_rows):
    e, n_rt, tr = idx.shape
    d = hs[0].shape[-1]
    ff = w1.shape[-1]
    tf = _row_tile(ff, 512)
    tn = _row_tile(d, 512)
    nf = ff // tf
    n_src = len(hs)
    tiles = lambda a: a.reshape(e * n_rt, 1, tr)
    smem = lambda imap: pl.BlockSpec((1, 1, tr), imap, memory_space=pltpu.SMEM)
    cur = lambda ei, ri, s: (ei * n_rt + ri, 0, 0)
    nxt = lambda ei, ri, s: (jnp.minimum(ei * n_rt + ri + 1, e * n_rt - 1), 0, 0)
    up = lambda ei, ri, s: (layer, ei, 0, jnp.minimum(s, nf - 1))
    down = lambda ei, ri, s: (layer, ei, 0, jnp.maximum(s - nf, 0))
    any_spec = pl.BlockSpec(memory_space=pl.ANY)
    return pl.pallas_call(
        functools.partial(_moe_ffn_kernel, n_rt=n_rt, part_rows=tuple(part_rows), nf=nf),
        grid=(e, n_rt, nf + d // tn),
        in_specs=[smem(cur), smem(nxt), pl.BlockSpec((1, tr, 1), lambda ei, ri, s: (ei, ri, 0))]
                 + [any_spec] * n_src
                 + [pl.BlockSpec((1, 1, d, tf), up), pl.BlockSpec((1, 1, d, tf), up),
                    pl.BlockSpec((1, 1, ff, tn), down)],
        out_specs=[pl.BlockSpec((1, n, tn), lambda ei, ri, s: (ei, ri, jnp.maximum(s - nf, 0))) for n in part_rows],
        out_shape=[jax.ShapeDtypeStruct((e, n * n_rt, d), F32) for n in part_rows],
        scratch_shapes=[pltpu.VMEM((tr, d), F32), pltpu.VMEM((tr, d), BF16), pltpu.VMEM((tr, ff), BF16),
                        pltpu.SemaphoreType.DMA],
        compiler_params=_params(3),
        name="moe_ffn",
    )(tiles(idx), tiles(idx), gates.reshape(e, n_rt * tr, 1), *hs, w1, w3, w2)


def _combine_kernel(lo_ref, sw_ref, x_ref, gt_ref, kept_ref, y_ref, fg_ref, o_ref, ybuf, acc, sem, *, cap, final):
    bi = pl.program_id(0)
    step = bi * pl.num_programs(1) + pl.program_id(1)
    n_steps = pl.num_programs(0) * pl.num_programs(1)
    ne = kept_ref.shape[2]
    w = MOE_WINDOW
    al = SUBLANES
    rows_all = y_ref.shape[1]

    def window_start(st, e, c):
        first = (st // pl.num_programs(1)) * cap + lo_ref[st * ne + e] // al * al + c * (w - al)
        return pl.multiple_of(jnp.minimum(first, rows_all - w), al)

    def fetch(st, c, slot, e):
        return pltpu.make_async_copy(y_ref.at[e, pl.ds(window_start(st, e, c), w)],
                                     ybuf.at[slot, pl.ds(e * w, w)], sem.at[slot])

    def start_all(st, c, slot):
        for e in range(ne):
            fetch(st, c, slot, e).start()

    def wait_all(st, c, slot):
        for e in range(ne):
            fetch(st, c, slot, e).wait()

    slot = lax.rem(step, 2)

    @pl.when(step == 0)
    def _():
        start_all(step, 0, slot)

    wait_all(step, 0, slot)

    @pl.when(step + 1 < n_steps)
    def _():
        start_all(step + 1, 0, 1 - slot)

    kept = kept_ref[0]
    lane = lax.broadcasted_iota(jnp.int32, (1, 2 * w), 1)
    first_half = lane < w
    row_in = jnp.where(first_half, lane, lane - w)
    acc[...] = jnp.zeros_like(acc)

    def add_windows(c, sl):
        span = jnp.where(c == sw_ref[step] - 1, rows_all, w - al)
        pieces = []
        for e2 in range(0, ne, 2):
            pos = jnp.where(first_half, kept[:, e2:e2 + 1], kept[:, e2 + 1:e2 + 2])
            off = jnp.where(first_half, window_start(step, e2, c), window_start(step, e2 + 1, c)) - bi * cap
            own_lo = jnp.where(first_half, lo_ref[step * ne + e2] // al * al,
                               lo_ref[step * ne + e2 + 1] // al * al) + c * (w - al)
            hit = (pos - off == row_in) & (pos >= own_lo) & (pos < own_lo + span)
            pieces.append(jnp.where(hit, 1.0, 0.0).astype(BF16))
        own = jnp.concatenate(pieces, axis=1)
        acc[...] += _dot(own, ybuf[sl].astype(BF16))

    add_windows(0, slot)

    def extra(c, carry):
        start_all(step, c, slot)
        wait_all(step, c, slot)
        add_windows(c, slot)
        return carry

    lax.fori_loop(1, sw_ref[step], extra, 0)
    xn = x_ref[0] + gt_ref[0] * acc[...]
    o_ref[0] = _rms_scale(xn) * fg_ref[...] if final else xn


def _combine(x, gt, kept_t, y, tables, final_g, final):
    b, t, d = x.shape
    ne = kept_t.shape[2]
    cap = EC_FACTOR * t // ne
    tm = _row_tile(t, MOE_TOKEN_TILE)
    lo, sweeps = tables
    tok = lambda wd: pl.BlockSpec((1, tm, wd), lambda bi, i, *_: (bi, i, 0))
    return pl.pallas_call(
        functools.partial(_combine_kernel, cap=cap, final=final),
        grid_spec=pltpu.PrefetchScalarGridSpec(
            num_scalar_prefetch=2,
            grid=(b, t // tm),
            in_specs=[tok(d), pl.BlockSpec((1, 1, d), lambda bi, i, *_: (bi, 0, 0)), tok(ne),
                      pl.BlockSpec(memory_space=pl.ANY), pl.BlockSpec(final_g.shape, lambda bi, i, *_: (0, 0))],
            out_specs=tok(d),
            scratch_shapes=[pltpu.VMEM((2, ne * MOE_WINDOW, d), F32), pltpu.VMEM((tm, d), F32),
                            pltpu.SemaphoreType.DMA((2,))]),
        out_shape=jax.ShapeDtypeStruct((b, t, d), F32),
        compiler_params=_params(2),
        name="moe_combine",
    )(lo, sweeps, x, gt, kept_t, y, final_g)


def _rope_tables(t):
    rows = t // GRID_W
    row = jnp.repeat(jnp.arange(rows, dtype=F32), GRID_W)
    col = jnp.tile(jnp.arange(GRID_W, dtype=F32), rows)
    inv = ROPE_BASE ** (-jnp.arange(0, ROPE_AXIS, 2, dtype=F32) / ROPE_AXIS)
    ar = row[:, None] * inv
    ac = col[:, None] * inv
    cos = jnp.concatenate([jnp.cos(ar), jnp.cos(ar), jnp.cos(ac), jnp.cos(ac)], axis=-1)
    sin = jnp.concatenate([jnp.sin(ar), jnp.sin(ar), jnp.sin(ac), jnp.sin(ac)], axis=-1)
    return jnp.concatenate([cos, sin], axis=-1)


def _rot_columns(w):
    half = ROPE_AXIS // 2
    return jnp.concatenate([-w[..., half:ROPE_AXIS], w[..., :half],
                            -w[..., ROPE_AXIS + half:], w[..., ROPE_AXIS:ROPE_AXIS + half]], axis=-1)


def _moe(streams, w1, w3, w2, layer):
    ne = w1.shape[1]
    n_rt = MOE_ROW_TILES
    idxs, gates, hs, infos = [], [], [], []
    for h, lg_t in streams:
        b, t, d = h.shape
        idx, gate, rank, kept = _route(lg_t)
        by_tile = lambda a: jnp.swapaxes(a, 0, 1).reshape(ne, n_rt, -1)
        idxs.append(by_tile(idx))
        gates.append(by_tile(gate))
        hs.append(h.reshape(b * t, d))
        infos.append((jnp.swapaxes(kept, 1, 2), _window_tables(rank, EC_FACTOR * t // ne, _row_tile(t, MOE_TOKEN_TILE))))
    ys = _moe_ffn(jnp.concatenate(idxs, axis=2), jnp.concatenate(gates, axis=2), hs, w1, w3, w2, layer,
                  [a.shape[2] for a in idxs])
    return [(y,) + info for y, info in zip(ys, infos)]


def kernel(x, c, ctx, c_ctx, w_ada, b_ada, norm_g, final_g, w_in_e, g_qa, w_uq, g_kv, w_ukv, w_pool, pool_scale,
           w_out_e, w_in_o, conv_w, conv_b, w_rg_a, b_rg_a, w_rg_x, b_rg_x, lru_lambda, w_out_o, w_router, w_e1,
           w_e3, w_e2):
    b, t, d = x.shape
    tc = ctx.shape[1]
    depth = w_ada.shape[0]
    n_heads = d // (2 * V_HEAD)
    pool_w = d // 2

    c_rows = jnp.concatenate([c, c_ctx[None], jnp.zeros((SUBLANES - b - 1, d), F32)], axis=0)
    mods = _mods(c_rows, w_ada, b_ada)

    def mod_rows(layer, k):
        m = mods[layer, :, k * d:(k + 1) * d]
        return m[:b, None, :], jnp.broadcast_to(m[b][None, None, :], (b, 1, d))

    cs_lat = _rope_tables(t)
    cs_ctx = jnp.concatenate([jnp.ones((tc, QK_ROPE), F32), jnp.zeros((tc, QK_ROPE), F32)], axis=-1)

    cs = ctx
    for layer in range(depth):
        last = layer == depth - 1
        (sh_m, csh_m), (sc_m, csc_m), (gt_m, cgt_m), (sh_f, csh_f), (sc_f, csc_f), (gt_f, cgt_f) = (
            mod_rows(layer, k) for k in range(N_MOD))
        g1 = norm_g[layer, 0][None]
        g2 = norm_g[layer, 1][None]
        if layer % 2 == 0:
            e = layer // 2
            wi = w_in_e[e]
            o_kpe = Q_LORA + KV_LORA
            kpe = wi[:, o_kpe:o_kpe + QK_ROPE]
            win = jnp.concatenate([wi[:, :o_kpe], wi[:, o_kpe + QK_ROPE:], kpe, _rot_columns(kpe)], axis=1).astype(BF16)
            wq3 = w_uq[e].reshape(Q_LORA, n_heads, QK_HEAD)
            wq = jnp.concatenate([wq3, _rot_columns(wq3[..., QK_NOPE:])], axis=-1).reshape(Q_LORA, n_heads * HEAD_PAD)
            wkv3 = w_ukv[e].reshape(KV_LORA, n_heads, QK_NOPE + V_HEAD)
            wk = wkv3[..., :QK_NOPE].reshape(KV_LORA, n_heads * QK_NOPE).astype(BF16)
            wv = wkv3[..., QK_NOPE:].reshape(KV_LORA, n_heads * V_HEAD).astype(BF16)
            pre = functools.partial(_even_pre, g=g1, win=win, gqa=g_qa[e][None], gkv=g_kv[e][None],
                                    wq=wq.astype(BF16), wk=wk, wv=wv, n_heads=n_heads)
            q_l, k_l, v_l, u_l = pre(x, sc_m, sh_m, cs_tab=cs_lat)
            q_c, k_c, v_c, u_c = pre(cs, csc_m, csh_m, cs_tab=cs_ctx)
            wo = w_out_e[e].astype(BF16)
            wo_a, wo_p = wo[:n_heads * V_HEAD], wo[n_heads * V_HEAD:]
            a_l = _attention(q_l, [(k_c, v_c), (k_l, v_l)], n_heads)
            pl_l = _pool(u_l, w_pool[e].astype(BF16), pool_scale[e][None])
            x, h_l, lg_l = _mix_out([a_l, pl_l], [wo_a, wo_p], x, gt_m, g2, sc_f, sh_f, w_router[layer].T)
            streams = [(h_l, lg_l)]
            if not last:
                a_c = _attention(q_c, [(k_c, v_c)], n_heads)
                pl_c = _pool(u_c, w_pool[e].astype(BF16), pool_scale[e][None])
                cs, h_c, lg_c = _mix_out([a_c, pl_c], [wo_a, wo_p], cs, cgt_m, g2, csc_f, csh_f, w_router[layer].T)
                streams.append((h_c, lg_c))
        else:
            o = layer // 2
            wi = w_in_o[o].astype(BF16)
            lw = wi.shape[1] // 2
            p_l = _norm_matmul(x, sc_m, sh_m, g1, wi)
            p_c = _norm_matmul(cs, csc_m, csh_m, g1, wi if not last else wi[:, lw:])
            wo = w_out_o[o].astype(BF16)
            y_l = _rglru(p_l, p_c[..., p_c.shape[-1] - lw:], conv_w[o], conv_b[o], w_rg_a[o].astype(BF16), b_rg_a[o],
                         w_rg_x[o].astype(BF16), b_rg_x[o], lru_lambda[o])
            x, h_l, lg_l = _mix_out([y_l], [wo], x, gt_m, g2, sc_f, sh_f, w_router[layer].T)
            streams = [(h_l, lg_l)]
            if not last:
                raise NotImplementedError("context output of an RG-LRU layer is only needed when it is not the last layer")
        outs = _moe(streams, w_e1, w_e3, w_e2, layer)
        y_l, kept_l, tab_l = outs[0]
        x = _combine(x, gt_f, kept_l, y_l, tab_l, final_g[None], final=last)
        if not last:
            y_c, kept_c, tab_c = outs[1]
            cs = _combine(cs, cgt_f, kept_c, y_c, tab_c, final_g[None], final=False)
    return x
```

```python
import functools
import math

import jax
import jax.numpy as jnp
from jax import lax
from jax.experimental import pallas as pl
from jax.experimental.pallas import tpu as pltpu

F32 = jnp.float32
BF16 = jnp.bfloat16

RMS_EPS = 1e-6
N_MOD = 6
GRID_W = 64
ROPE_BASE = 10000.0

V_HEAD = 128
QK_NOPE = 128
QK_ROPE = 64
QK_HEAD = QK_NOPE + QK_ROPE
Q_LORA = 512
KV_LORA = 512
ROPE_AXIS = QK_ROPE // 2
HEAD_PAD = 256
KEY_CHUNK = 256

POOL_WINDOWS = (2, 4, 8, 16)
POOL_PAD = 16

LRU_BLOCKS = 8
CONV_W = 4
CONV_PAD = 8
LRU_C = 8.0

N_EXPERTS = 16
EC_FACTOR = 2
MOE_ROW_TILES = 2
MOE_TOKEN_TILE = 256
MOE_WINDOW = 64

DMA_ISSUE_UNROLL = 8
SCAN_UNROLL = 8

SUBLANES = 8
LANES = 128
VMEM_LIMIT_BYTES = 56 * 1024 * 1024


def _params(n_grid):
    return pltpu.CompilerParams(dimension_semantics=("arbitrary",) * n_grid,
                                vmem_limit_bytes=VMEM_LIMIT_BYTES)


def _const_spec(shape):
    nd = len(shape)
    return pl.BlockSpec(shape, lambda *_: (0,) * nd, pipeline_mode=pl.Buffered(1))


def _rms_scale(xf):
    return xf * lax.rsqrt(jnp.mean(xf * xf, axis=-1, keepdims=True) + RMS_EPS)


def _dot(a, b):
    return jnp.dot(a, b, preferred_element_type=F32)


def _dot_nt(a, b):
    return lax.dot_general(a, b, (((1,), (1,)), ((), ())), preferred_element_type=F32)


def _row_tile(t, pref):
    return pref if t % pref == 0 else t


def _mods_kernel(c_ref, w_ref, b_ref, o_ref):
    c = c_ref[...]
    s = c * jax.nn.sigmoid(c)
    o_ref[0] = _dot(s.astype(BF16), w_ref[0].astype(BF16)) + b_ref[0]


def _mods(c_rows, w_ada, b_ada):
    depth, d, n = w_ada.shape
    tn = 1024
    return pl.pallas_call(
        _mods_kernel,
        grid=(depth, n // tn),
        in_specs=[_const_spec(c_rows.shape),
                  pl.BlockSpec((1, d, tn), lambda l, j: (l, 0, j)),
                  pl.BlockSpec((1, 1, tn), lambda l, j: (l, 0, j))],
        out_specs=pl.BlockSpec((1, c_rows.shape[0], tn), lambda l, j: (l, 0, j)),
        out_shape=jax.ShapeDtypeStruct((depth, c_rows.shape[0], n), F32),
        compiler_params=_params(2),
        name="adaln_mods",
    )(c_rows, w_ada, b_ada.reshape(depth, 1, n))


def _even_pre_kernel(x_ref, sc_ref, sh_ref, g_ref, win_ref, gqa_ref, gkv_ref, wq_ref, wk_ref, wv_ref, cs_ref,
                     q_ref, k_ref, v_ref, u_ref, *, n_heads, scale, pool_w):
    h = _rms_scale(x_ref[0]) * (g_ref[...] * (1.0 + sc_ref[0])) + sh_ref[0]
    p = _dot(h.astype(BF16), win_ref[...])
    cs = cs_ref[...]
    o_pool = Q_LORA + KV_LORA
    o_rope = o_pool + pool_w

    ql = _rms_scale(p[:, :Q_LORA]) * gqa_ref[...]
    q = _dot(ql.astype(BF16), wq_ref[...])
    for hd in range(n_heads):
        base = hd * HEAD_PAD
        hi = q[:, base + QK_NOPE:base + HEAD_PAD] * cs
        hi = hi + pltpu.roll(hi, QK_ROPE, axis=1)
        q_ref[0, :, base:base + QK_NOPE] = (q[:, base:base + QK_NOPE] * scale).astype(BF16)
        q_ref[0, :, base + QK_NOPE:base + HEAD_PAD] = (hi * scale).astype(BF16)

    kvl = (_rms_scale(p[:, Q_LORA:o_pool]) * gkv_ref[...]).astype(BF16)
    kn = _dot(kvl, wk_ref[...])
    v_ref[0] = _dot(kvl, wv_ref[...]).astype(BF16)
    kp = p[:, o_rope:o_rope + 2 * QK_ROPE] * cs
    kp = kp + pltpu.roll(kp, QK_ROPE, axis=1)
    lane = lax.broadcasted_iota(jnp.int32, kp.shape, 1)
    kp = jnp.where(lane < QK_ROPE, kp, 0.0).astype(BF16)
    for hd in range(n_heads):
        base = hd * HEAD_PAD
        k_ref[0, :, base:base + QK_NOPE] = kn[:, hd * QK_NOPE:(hd + 1) * QK_NOPE].astype(BF16)
        k_ref[0, :, base + QK_NOPE:base + HEAD_PAD] = kp
    u_ref[0] = p[:, o_pool:o_rope]


def _even_pre(x, sc, sh, g, win, gqa, gkv, wq, wk, wv, cs_tab, n_heads):
    b, t, d = x.shape
    tm = _row_tile(t, 256)
    pool_w = d // 2
    kern = functools.partial(_even_pre_kernel, n_heads=n_heads, scale=QK_HEAD ** -0.5 * math.log2(math.e),
                             pool_w=pool_w)
    tok = lambda w: pl.BlockSpec((1, tm, w), lambda bi, i: (bi, i, 0))
    per_b = pl.BlockSpec((1, 1, d), lambda bi, i: (bi, 0, 0))
    return pl.pallas_call(
        kern,
        grid=(b, t // tm),
        in_specs=[tok(d), per_b, per_b, _const_spec(g.shape), _const_spec(win.shape), _const_spec(gqa.shape),
                  _const_spec(gkv.shape), _const_spec(wq.shape), _const_spec(wk.shape), _const_spec(wv.shape),
                  pl.BlockSpec((tm, 2 * QK_ROPE), lambda bi, i: (i, 0))],
        out_specs=[tok(n_heads * HEAD_PAD), tok(n_heads * HEAD_PAD), tok(n_heads * V_HEAD), tok(pool_w)],
        out_shape=[jax.ShapeDtypeStruct((b, t, n_heads * HEAD_PAD), BF16),
                   jax.ShapeDtypeStruct((b, t, n_heads * HEAD_PAD), BF16),
                   jax.ShapeDtypeStruct((b, t, n_heads * V_HEAD), BF16),
                   jax.ShapeDtypeStruct((b, t, pool_w), F32)],
        compiler_params=_params(2),
        name="even_pre",
    )(x, sc, sh, g, win, gqa, gkv, wq, wk, wv, cs_tab)


def _attn_kernel(*refs, chunks):
    q_ref, o_ref = refs[0], refs[-1]
    q = q_ref[0]
    m = l = acc = None
    for i, s0, n in chunks:
        s = _dot_nt(q, refs[1 + 2 * i][0, s0:s0 + n, :])
        v = refs[2 + 2 * i][0, s0:s0 + n, :]
        mc = jnp.max(s, axis=-1, keepdims=True)
        if m is None:
            m_new = mc
            p = jnp.exp2(s - m_new)
            l = jnp.sum(p, axis=-1, keepdims=True)
            acc = _dot(p.astype(BF16), v)
        else:
            m_new = jnp.maximum(m, mc)
            alpha = jnp.exp2(m - m_new)
            p = jnp.exp2(s - m_new)
            l = alpha * l + jnp.sum(p, axis=-1, keepdims=True)
            acc = alpha * acc + _dot(p.astype(BF16), v)
        m = m_new
    o_ref[0] = (acc * (1.0 / l)).astype(BF16)


def _attention(q, kvs, n_heads):
    b, tq_all, _ = q.shape
    tq = _row_tile(tq_all, 2048)
    in_specs = [pl.BlockSpec((1, tq, HEAD_PAD), lambda bi, hd, i: (bi, i, hd))]
    args = [q]
    chunks = []
    for j, (k, v) in enumerate(kvs):
        s = k.shape[1]
        tk = _row_tile(s, KEY_CHUNK)
        chunks += [(j, s0, tk) for s0 in range(0, s, tk)]
        in_specs.append(pl.BlockSpec((1, s, HEAD_PAD), lambda bi, hd, i: (bi, 0, hd)))
        in_specs.append(pl.BlockSpec((1, s, V_HEAD), lambda bi, hd, i: (bi, 0, hd)))
        args += [k, v]
    return pl.pallas_call(
        functools.partial(_attn_kernel, chunks=tuple(chunks)),
        grid=(b, n_heads, tq_all // tq),
        in_specs=in_specs,
        out_specs=pl.BlockSpec((1, tq, V_HEAD), lambda bi, hd, i: (bi, i, hd)),
        out_shape=jax.ShapeDtypeStruct((b, tq_all, n_heads * V_HEAD), BF16),
        compiler_params=_params(3),
        name="mla_attention",
    )(*args)


def _pool_kernel(u_ref, w_ref, ps_ref, o_ref, e0, e1, *, t, windows):
    gi = pl.program_id(1)
    gc = u_ref.shape[-1]
    u = u_ref[0]
    zpad = jnp.zeros((POOL_PAD, gc), F32)
    for e in (e0, e1):
        e[0:POOL_PAD, :] = zpad
        e[POOL_PAD + t:2 * POOL_PAD + t, :] = zpad
    e0[POOL_PAD:POOL_PAD + t, :] = u
    pos = lax.broadcasted_iota(jnp.int32, (t, 1), 0)
    n = t + 2 * POOL_PAD - 2 * SUBLANES
    for k, w in enumerate(windows):
        @pl.when(gi == k)
        def _(w=w):
            src, dst = e0, e1
            dst[SUBLANES:SUBLANES + n, :] = src[SUBLANES - 1:SUBLANES - 1 + n, :] + src[SUBLANES:SUBLANES + n, :]
            src, dst = dst, src
            m = 2
            while m < w:
                hm = m // 2
                dst[SUBLANES:SUBLANES + n, :] = (src[SUBLANES - hm:SUBLANES - hm + n, :]
                                                 + src[SUBLANES + hm:SUBLANES + hm + n, :])
                src, dst = dst, src
                m *= 2
            tot = src[POOL_PAD:POOL_PAD + t, :]
            cnt = (jnp.minimum(pos + (w - w // 2), t) - jnp.maximum(pos - w // 2, 0)).astype(F32)
            pooled = tot / cnt - u
            o_ref[0] = (_dot(pooled.astype(BF16), w_ref[0]) * ps_ref[...]).astype(BF16)


def _pool(u, w_pool, pool_scale):
    b, t, pw = u.shape
    ng = len(POOL_WINDOWS)
    gc = pw // ng
    return pl.pallas_call(
        functools.partial(_pool_kernel, t=t, windows=POOL_WINDOWS),
        grid=(b, ng),
        in_specs=[pl.BlockSpec((1, t, gc), lambda bi, g: (bi, 0, g)),
                  pl.BlockSpec((1, gc, gc), lambda bi, g: (g, 0, 0)),
                  pl.BlockSpec((1, gc), lambda bi, g: (0, g))],
        out_specs=pl.BlockSpec((1, t, gc), lambda bi, g: (bi, 0, g)),
        out_shape=jax.ShapeDtypeStruct((b, t, pw), BF16),
        scratch_shapes=[pltpu.VMEM((t + 2 * POOL_PAD, gc), F32), pltpu.VMEM((t + 2 * POOL_PAD, gc), F32)],
        compiler_params=_params(2),
        name="pool_mixer",
    )(u, w_pool, pool_scale)


def _mix_out_kernel(*refs, n_a):
    a_refs, w_refs = refs[:n_a], refs[n_a:2 * n_a]
    x_ref, gt_ref, g2_ref, sc_ref, sh_ref, wr_ref, xo_ref, h_ref, lg_ref = refs[2 * n_a:]
    m = functools.reduce(jnp.add, [_dot(a_refs[i][0], w_refs[i][...]) for i in range(n_a)])
    xn = x_ref[0] + gt_ref[0] * m
    xo_ref[0] = xn
    hf = _rms_scale(xn) * (g2_ref[...] * (1.0 + sc_ref[0])) + sh_ref[0]
    h_ref[0] = hf
    hb = hf.astype(BF16)
    h_lo = (hf - hb.astype(F32)).astype(BF16)
    wr = wr_ref[...]
    w_hi = wr.astype(BF16)
    w_lo = (wr - w_hi.astype(F32)).astype(BF16)
    lg_ref[0] = _dot_nt(w_hi, hb) + (_dot_nt(w_lo, hb) + _dot_nt(w_hi, h_lo))


def _mix_out(a_list, w_list, x, gt, g2, sc, sh, w_router_t):
    b, t, d = x.shape
    tm = _row_tile(t, 512)
    ne = w_router_t.shape[0]
    tok = lambda w: pl.BlockSpec((1, tm, w), lambda bi, i: (bi, i, 0))
    per_b = pl.BlockSpec((1, 1, d), lambda bi, i: (bi, 0, 0))
    in_specs = ([tok(a.shape[-1]) for a in a_list] + [_const_spec(w.shape) for w in w_list]
                + [tok(d), per_b, _const_spec(g2.shape), per_b, per_b, _const_spec(w_router_t.shape)])
    return pl.pallas_call(
        functools.partial(_mix_out_kernel, n_a=len(a_list)),
        grid=(b, t // tm),
        in_specs=in_specs,
        out_specs=[tok(d), tok(d), pl.BlockSpec((1, ne, tm), lambda bi, i: (bi, 0, i))],
        out_shape=[jax.ShapeDtypeStruct((b, t, d), F32), jax.ShapeDtypeStruct((b, t, d), F32),
                   jax.ShapeDtypeStruct((b, ne, t), F32)],
        compiler_params=_params(2),
        name="mix_out",
    )(*a_list, *w_list, x, gt, g2, sc, sh, w_router_t)


def _norm_matmul_kernel(x_ref, sc_ref, sh_ref, g_ref, w_ref, o_ref):
    h = _rms_scale(x_ref[0]) * (g_ref[...] * (1.0 + sc_ref[0])) + sh_ref[0]
    o_ref[0] = _dot(h.astype(BF16), w_ref[...])


def _norm_matmul(x, sc, sh, g, w):
    b, t, d = x.shape
    n = w.shape[1]
    tm = _row_tile(t, 512)
    tn = _row_tile(n, 2048)
    return pl.pallas_call(
        _norm_matmul_kernel,
        grid=(n // tn, b, t // tm),
        in_specs=[pl.BlockSpec((1, tm, d), lambda j, bi, i: (bi, i, 0)),
                  pl.BlockSpec((1, 1, d), lambda j, bi, i: (bi, 0, 0)),
                  pl.BlockSpec((1, 1, d), lambda j, bi, i: (bi, 0, 0)),
                  _const_spec(g.shape),
                  pl.BlockSpec((d, tn), lambda j, bi, i: (0, j))],
        out_specs=pl.BlockSpec((1, tm, tn), lambda j, bi, i: (bi, i, j)),
        out_shape=jax.ShapeDtypeStruct((b, t, n), F32),
        compiler_params=_params(3),
        name="norm_matmul",
    )(x, sc, sh, g, w)


def _rglru_kernel(gate_ref, rec_ref, recc_ref, cw_ref, cb_ref, wa_ref, ba_ref, wx_ref, bx_ref, lam_ref, o_ref,
                  ext, extc, ubuf, ubufc, abuf, bbuf, ybuf, *, t, tc, chunk):
    c = rec_ref.shape[-1]
    zpad = jnp.zeros((CONV_PAD, c), F32)
    for e, r, n in ((ext, rec_ref, t), (extc, recc_ref, tc)):
        e[0:CONV_PAD, :] = zpad
        e[CONV_PAD + n:2 * CONV_PAD + n, :] = zpad
        e[CONV_PAD:CONV_PAD + n, :] = r[0]
    cw = cw_ref[...]
    cb = cb_ref[...]
    row = lax.broadcasted_iota(jnp.int32, (SUBLANES, c), 0)

    def gates(e_ref, u_ref, r0, n, d):
        if d == 0:
            u = cb
            for k in range(CONV_W):
                off = CONV_PAD + r0 + k - CONV_W // 2
                u = u + cw[k:k + 1] * e_ref[off:off + n, :]
            u_ref[r0:r0 + n, :] = u
        else:
            u = u_ref[r0:r0 + n, :]
        ub = u.astype(BF16)
        r = jax.nn.sigmoid(_dot(ub, wa_ref[d, 0]) + ba_ref[d:d + 1])
        i = jax.nn.sigmoid(_dot(ub, wx_ref[d, 0]) + bx_ref[d:d + 1])
        log_a = (-LRU_C * r) * jax.nn.softplus(-lam_ref[d:d + 1])
        a = jnp.exp(log_a)
        abuf[0:n, :] = a
        bbuf[0:n, :] = jnp.sqrt(1.0 - a * a) * (i * u)

    def scan(n, carry, d, emit):
        nb = n // SUBLANES

        def body(j, hc):
            jj = j if d == 0 else nb - 1 - j
            off = pl.multiple_of(jj * SUBLANES, SUBLANES)
            a = abuf[pl.ds(off, SUBLANES), :]
            bb = bbuf[pl.ds(off, SUBLANES), :]
            for s in (1, 2, 4):
                if d == 0:
                    sh, msk = s, row >= s
                else:
                    sh, msk = SUBLANES - s, row < SUBLANES - s
                a_s = jnp.where(msk, pltpu.roll(a, sh, axis=0), 1.0)
                b_s = jnp.where(msk, pltpu.roll(bb, sh, axis=0), 0.0)
                bb = a * b_s + bb
                a = a * a_s
            h = a * hc + bb
            emit(off, h)
            return h[SUBLANES - 1:SUBLANES] if d == 0 else h[0:1]

        return lax.fori_loop(0, nb, body, carry, unroll=SCAN_UNROLL)

    def run(e_ref, u_ref, n_all, d, carry, emit):
        cn = min(chunk, n_all)
        starts = list(range(0, n_all, cn))
        for r0 in (starts if d == 0 else starts[::-1]):
            gates(e_ref, u_ref, r0, cn, d)
            carry = scan(cn, carry, d, functools.partial(emit, r0))
        return carry

    def emit_none(r0, off, h):
        pass

    def emit_set(r0, off, h):
        ybuf[pl.ds(r0 + off, SUBLANES), :] = h

    def emit_add(r0, off, h):
        ybuf[pl.ds(r0 + off, SUBLANES), :] += h

    h0 = jnp.zeros((1, c), F32)
    run(ext, ubuf, t, 0, run(extc, ubufc, tc, 0, h0, emit_none), emit_set)
    run(ext, ubuf, t, 1, run(extc, ubufc, tc, 1, h0, emit_none), emit_add)
    cn = min(chunk, t)
    for r0 in range(0, t, cn):
        o_ref[0, r0:r0 + cn, :] = (ybuf[r0:r0 + cn, :] * jax.nn.gelu(gate_ref[0, r0:r0 + cn, :])).astype(BF16)


def _rglru(p_lat, p_ctx, conv_w, conv_b, w_a, b_a, w_x, b_x, lam):
    b, t, c2 = p_lat.shape
    cw = c2 // 2
    tc = p_ctx.shape[1]
    nb = w_a.shape[1]
    bs = cw // nb
    chunk = 512
    vec = lambda rows: pl.BlockSpec((rows, bs), lambda bi, n: (0, n))
    wspec = pl.BlockSpec((2, 1, bs, bs), lambda bi, n: (0, n, 0, 0))
    return pl.pallas_call(
        functools.partial(_rglru_kernel, t=t, tc=tc, chunk=chunk),
        grid=(b, nb),
        in_specs=[pl.BlockSpec((1, t, bs), lambda bi, n: (bi, 0, n)),
                  pl.BlockSpec((1, t, bs), lambda bi, n: (bi, 0, nb + n)),
                  pl.BlockSpec((1, tc, bs), lambda bi, n: (bi, 0, n)),
                  vec(CONV_W), vec(1), wspec, vec(2), wspec, vec(2), vec(2)],
        out_specs=pl.BlockSpec((1, t, bs), lambda bi, n: (bi, 0, n)),
        out_shape=jax.ShapeDtypeStruct((b, t, cw), BF16),
        scratch_shapes=[pltpu.VMEM((t + 2 * CONV_PAD, bs), F32), pltpu.VMEM((tc + 2 * CONV_PAD, bs), F32),
                        pltpu.VMEM((t, bs), F32), pltpu.VMEM((tc, bs), F32),
                        pltpu.VMEM((min(chunk, max(t, tc)), bs), F32), pltpu.VMEM((min(chunk, max(t, tc)), bs), F32),
                        pltpu.VMEM((t, bs), F32)],
        compiler_params=_params(2),
        name="rglru",
    )(p_lat, p_lat, p_ctx, conv_w, conv_b.reshape(1, cw), w_a, b_a, w_x, b_x, lam)


def _lane_excl_cumsum(x, tri, ones):
    r, t = x.shape
    run = jnp.zeros((r, LANES), F32)
    outs = []
    for k in range(t // LANES):
        blk = x[:, k * LANES:(k + 1) * LANES]
        outs.append(_dot(blk, tri) + run)
        run = run + _dot(blk, ones)
    return jnp.concatenate(outs, axis=1)


def _route_kernel(lg_ref, idx_ref, gate_ref, rank_ref, kept_ref, *, t, cap):
    bi = pl.program_id(0)
    lg = lg_ref[0]
    ne = lg.shape[0]
    ex = jnp.exp(lg - jnp.max(lg, axis=0, keepdims=True))
    aff = ex / jnp.sum(ex, axis=0, keepdims=True)

    thr = jnp.zeros((ne, 1), jnp.int32)
    for bit in range(30, -1, -1):
        cand = thr | (1 << bit)
        n_ge = jnp.sum(jnp.where(aff >= lax.bitcast_convert_type(cand, F32), 1.0, 0.0), axis=1, keepdims=True)
        thr = jnp.where(n_ge >= cap, cand, thr)
    thr_f = lax.bitcast_convert_type(thr, F32)
    gt = aff > thr_f
    eq = aff == thr_f
    need = cap - jnp.sum(jnp.where(gt, 1.0, 0.0), axis=1, keepdims=True)

    ri = lax.broadcasted_iota(jnp.int32, (LANES, LANES), 0)
    ci = lax.broadcasted_iota(jnp.int32, (LANES, LANES), 1)
    tri = jnp.where(ri < ci, 1.0, 0.0).astype(BF16)
    ones = jnp.ones((LANES, LANES), BF16)
    eq_rank = _lane_excl_cumsum(jnp.where(eq, 1.0, 0.0).astype(BF16), tri, ones)
    sel = gt | (eq & (eq_rank < need))
    rank = _lane_excl_cumsum(jnp.where(sel, 1.0, 0.0).astype(BF16), tri, ones).astype(jnp.int32)
    kept = jnp.where(sel, rank, -1)
    rank_ref[0] = rank
    kept_ref[0] = kept

    tpos = lax.broadcasted_iota(jnp.int32, (1, t), 1)
    t_hi = (tpos >> 6).astype(F32)
    t_lo = (tpos & 63).astype(F32)
    g_hi = aff.astype(BF16).astype(F32)
    g_mid = (aff - g_hi).astype(BF16).astype(F32)
    g_lo = (aff - g_hi) - g_mid
    zrows = jnp.zeros((2 * SUBLANES - 5, t), F32)
    p_iota = lax.broadcasted_iota(jnp.int32, (cap, t), 0)
    for e in range(ne):
        onehot = jnp.where(kept[e:e + 1, :] == p_iota, 1.0, 0.0).astype(BF16)
        vals = jnp.concatenate([t_hi, t_lo, g_hi[e:e + 1], g_mid[e:e + 1], g_lo[e:e + 1], zrows], axis=0).astype(BF16)
        out = _dot_nt(vals, onehot)
        idx_ref[0, e:e + 1, :] = (out[0:1] * 64.0 + out[1:2]).astype(jnp.int32) + bi * t
        gate_ref[0, e:e + 1, :] = (out[2:3] + out[3:4]) + out[4:5]


def _route(lg_t):
    b, ne, t = lg_t.shape
    cap = EC_FACTOR * t // ne
    lst = lambda dt: jax.ShapeDtypeStruct((b, ne, cap), dt)
    dense = jax.ShapeDtypeStruct((b, ne, t), jnp.int32)
    per_b = lambda s: pl.BlockSpec((1,) + s, lambda bi: (bi, 0, 0))
    return pl.pallas_call(
        functools.partial(_route_kernel, t=t, cap=cap),
        grid=(b,),
        in_specs=[per_b((ne, t))],
        out_specs=[per_b((ne, cap)), per_b((ne, cap)), per_b((ne, t)), per_b((ne, t))],
        out_shape=[lst(jnp.int32), lst(F32), dense, dense],
        compiler_params=_params(1),
        name="moe_route",
    )(lg_t)


def _window_tables(rank, cap, tm):
    b, ne, t = rank.shape
    lo = jnp.swapaxes(rank[:, :, ::tm], 1, 2)
    hi = jnp.concatenate([lo[:, 1:], jnp.full((b, 1, ne), cap, jnp.int32)], axis=1)
    span = hi - lo // SUBLANES * SUBLANES
    step = MOE_WINDOW - SUBLANES
    sweeps = jnp.maximum(jnp.max((span + step - 1) // step, axis=2), 1)
    return lo.reshape(-1), sweeps.reshape(-1)


def _moe_ffn_kernel(*refs, n_rt, part_rows, nf):
    n_src = len(part_rows)
    idx_ref, idxn_ref, g_ref = refs[:3]
    h_refs = refs[3:3 + n_src]
    w1_ref, w3_ref, w2_ref = refs[3 + n_src:6 + n_src]
    y_refs = refs[6 + n_src:6 + 2 * n_src]
    stage, xb, hmid, sem = refs[6 + 2 * n_src:]
    tr = sum(part_rows)
    tf = w1_ref.shape[-1]
    s = pl.program_id(2)
    tile = pl.program_id(0) * n_rt + pl.program_id(1)
    n_tiles = pl.num_programs(0) * n_rt
    starts = [sum(part_rows[:i]) for i in range(n_src)]

    def gather(ix_ref):
        for lo, n, h_ref in zip(starts, part_rows, h_refs):
            def issue(p, c, h_ref=h_ref):
                pltpu.make_async_copy(h_ref.at[pl.ds(ix_ref[0, 0, p], 1)], stage.at[pl.ds(p, 1)], sem).start()
                return c
            lax.fori_loop(lo, lo + n, issue, 0, unroll=DMA_ISSUE_UNROLL)

    @pl.when(s == 0)
    def _():
        @pl.when(tile == 0)
        def _():
            gather(idx_ref)
        pltpu.make_async_copy(h_refs[0].at[pl.ds(0, tr)], stage, sem).wait()
        xb[...] = stage[...].astype(BF16)

        @pl.when(tile + 1 < n_tiles)
        def _():
            gather(idxn_ref)

    @pl.when(s < nf)
    def _():
        x = xb[...]
        a1 = _dot(x, w1_ref[0, 0].astype(BF16))
        a3 = _dot(x, w3_ref[0, 0].astype(BF16))
        hm = (a1 * jax.nn.sigmoid(a1) * a3).astype(BF16)
        for f in range(nf):
            @pl.when(s == f)
            def _(f=f):
                hmid[:, f * tf:(f + 1) * tf] = hm

    @pl.when(s >= nf)
    def _():
        tn = w2_ref.shape[-1]
        y = _dot(hmid[...], w2_ref[0, 0].astype(BF16)) * g_ref[0]
        for c in range(y_refs[0].shape[-1] // tn):
            @pl.when(s == nf + c)
            def _(c=c):
                for lo, n, y_ref in zip(starts, part_rows, y_refs):
                    y_ref[0, :, c * tn:(c + 1) * tn] = y[lo:lo + n, :]


def _moe_ffn(idx, gates, hs, w1, w3, w2, layer, part_rows):
    e, n_rt, tr = idx.shape
    d = hs[0].shape[-1]
    ff = w1.shape[-1]
    tf = _row_tile(ff, 256)
    tn = _row_tile(d, 256)
    nf = ff // tf
    n_src = len(hs)
    tiles = lambda a: a.reshape(e * n_rt, 1, tr)
    smem = lambda imap: pl.BlockSpec((1, 1, tr), imap, memory_space=pltpu.SMEM)
    cur = lambda ei, ri, s: (ei * n_rt + ri, 0, 0)
    nxt = lambda ei, ri, s: (jnp.minimum(ei * n_rt + ri + 1, e * n_rt - 1), 0, 0)
    up = lambda ei, ri, s: (layer, ei, 0, jnp.minimum(s, nf - 1))
    down = lambda ei, ri, s: (layer, ei, 0, jnp.maximum(s - nf, 0))
    any_spec = pl.BlockSpec(memory_space=pl.ANY)
    return pl.pallas_call(
        functools.partial(_moe_ffn_kernel, n_rt=n_rt, part_rows=tuple(part_rows), nf=nf),
        grid=(e, n_rt, nf + d // tn),
        in_specs=[smem(cur), smem(nxt), pl.BlockSpec((1, tr, 1), lambda ei, ri, s: (ei, ri, 0))]
                 + [any_spec] * n_src
                 + [pl.BlockSpec((1, 1, d, tf), up), pl.BlockSpec((1, 1, d, tf), up),
                    pl.BlockSpec((1, 1, ff, tn), down)],
        out_specs=[pl.BlockSpec((1, n, d), lambda ei, ri, s: (ei, ri, 0)) for n in part_rows],
        out_shape=[jax.ShapeDtypeStruct((e, n * n_rt, d), F32) for n in part_rows],
        scratch_shapes=[pltpu.VMEM((tr, d), F32), pltpu.VMEM((tr, d), BF16), pltpu.VMEM((tr, ff), BF16),
                        pltpu.SemaphoreType.DMA],
        compiler_params=_params(3),
        name="moe_ffn",
    )(tiles(idx), tiles(idx), gates.reshape(e, n_rt * tr, 1), *hs, w1, w3, w2)


def _combine_kernel(lo_ref, sw_ref, x_ref, gt_ref, kept_ref, y_ref, fg_ref, o_ref, ybuf, acc, sem, *, cap, final):
    bi = pl.program_id(0)
    step = bi * pl.num_programs(1) + pl.program_id(1)
    n_steps = pl.num_programs(0) * pl.num_programs(1)
    ne = kept_ref.shape[2]
    w = MOE_WINDOW
    al = SUBLANES
    rows_all = y_ref.shape[1]

    def window_start(st, e, c):
        first = (st // pl.num_programs(1)) * cap + lo_ref[st * ne + e] // al * al + c * (w - al)
        return pl.multiple_of(jnp.minimum(first, rows_all - w), al)

    def fetch(st, c, slot, e):
        return pltpu.make_async_copy(y_ref.at[e, pl.ds(window_start(st, e, c), w)],
                                     ybuf.at[slot, pl.ds(e * w, w)], sem.at[slot])

    def start_all(st, c, slot):
        for e in range(ne):
            fetch(st, c, slot, e).start()

    def wait_all(st, c, slot):
        for e in range(ne):
            fetch(st, c, slot, e).wait()

    slot = lax.rem(step, 2)

    @pl.when(step == 0)
    def _():
        start_all(step, 0, slot)

    wait_all(step, 0, slot)

    @pl.when(step + 1 < n_steps)
    def _():
        start_all(step + 1, 0, 1 - slot)

    kept = kept_ref[0]
    lane = lax.broadcasted_iota(jnp.int32, (1, 2 * w), 1)
    first_half = lane < w
    row_in = jnp.where(first_half, lane, lane - w)
    acc[...] = jnp.zeros_like(acc)

    def add_windows(c, sl):
        span = jnp.where(c == sw_ref[step] - 1, rows_all, w - al)
        pieces = []
        for e2 in range(0, ne, 2):
            pos = jnp.where(first_half, kept[:, e2:e2 + 1], kept[:, e2 + 1:e2 + 2])
            off = jnp.where(first_half, window_start(step, e2, c), window_start(step, e2 + 1, c)) - bi * cap
            own_lo = jnp.where(first_half, lo_ref[step * ne + e2] // al * al,
                               lo_ref[step * ne + e2 + 1] // al * al) + c * (w - al)
            hit = (pos - off == row_in) & (pos >= own_lo) & (pos < own_lo + span)
            pieces.append(jnp.where(hit, 1.0, 0.0).astype(BF16))
        own = jnp.concatenate(pieces, axis=1)
        acc[...] += _dot(own, ybuf[sl].astype(BF16))

    add_windows(0, slot)

    def extra(c, carry):
        start_all(step, c, slot)
        wait_all(step, c, slot)
        add_windows(c, slot)
        return carry

    lax.fori_loop(1, sw_ref[step], extra, 0)
    xn = x_ref[0] + gt_ref[0] * acc[...]
    o_ref[0] = _rms_scale(xn) * fg_ref[...] if final else xn


def _combine(x, gt, kept_t, y, tables, final_g, final):
    b, t, d = x.shape
    ne = kept_t.shape[2]
    cap = EC_FACTOR * t // ne
    tm = _row_tile(t, MOE_TOKEN_TILE)
    lo, sweeps = tables
    tok = lambda wd: pl.BlockSpec((1, tm, wd), lambda bi, i, *_: (bi, i, 0))
    return pl.pallas_call(
        functools.partial(_combine_kernel, cap=cap, final=final),
        grid_spec=pltpu.PrefetchScalarGridSpec(
            num_scalar_prefetch=2,
            grid=(b, t // tm),
            in_specs=[tok(d), pl.BlockSpec((1, 1, d), lambda bi, i, *_: (bi, 0, 0)), tok(ne),
                      pl.BlockSpec(memory_space=pl.ANY), pl.BlockSpec(final_g.shape, lambda bi, i, *_: (0, 0))],
            out_specs=tok(d),
            scratch_shapes=[pltpu.VMEM((2, ne * MOE_WINDOW, d), F32), pltpu.VMEM((tm, d), F32),
                            pltpu.SemaphoreType.DMA((2,))]),
        out_shape=jax.ShapeDtypeStruct((b, t, d), F32),
        compiler_params=_params(2),
        name="moe_combine",
    )(lo, sweeps, x, gt, kept_t, y, final_g)


def _rope_tables(t):
    rows = t // GRID_W
    row = jnp.repeat(jnp.arange(rows, dtype=F32), GRID_W)
    col = jnp.tile(jnp.arange(GRID_W, dtype=F32), rows)
    inv = ROPE_BASE ** (-jnp.arange(0, ROPE_AXIS, 2, dtype=F32) / ROPE_AXIS)
    ar = row[:, None] * inv
    ac = col[:, None] * inv
    cos = jnp.concatenate([jnp.cos(ar), jnp.cos(ar), jnp.cos(ac), jnp.cos(ac)], axis=-1)
    sin = jnp.concatenate([jnp.sin(ar), jnp.sin(ar), jnp.sin(ac), jnp.sin(ac)], axis=-1)
    return jnp.concatenate([cos, sin], axis=-1)


def _rot_columns(w):
    half = ROPE_AXIS // 2
    return jnp.concatenate([-w[..., half:ROPE_AXIS], w[..., :half],
                            -w[..., ROPE_AXIS + half:], w[..., ROPE_AXIS:ROPE_AXIS + half]], axis=-1)


def _moe(streams, w1, w3, w2, layer):
    ne = w1.shape[1]
    n_rt = MOE_ROW_TILES
    idxs, gates, hs, infos = [], [], [], []
    for h, lg_t in streams:
        b, t, d = h.shape
        idx, gate, rank, kept = _route(lg_t)
        by_tile = lambda a: jnp.swapaxes(a, 0, 1).reshape(ne, n_rt, -1)
        idxs.append(by_tile(idx))
        gates.append(by_tile(gate))
        hs.append(h.reshape(b * t, d))
        infos.append((jnp.swapaxes(kept, 1, 2), _window_tables(rank, EC_FACTOR * t // ne, _row_tile(t, MOE_TOKEN_TILE))))
    ys = _moe_ffn(jnp.concatenate(idxs, axis=2), jnp.concatenate(gates, axis=2), hs, w1, w3, w2, layer,
                  [a.shape[2] for a in idxs])
    return [(y,) + info for y, info in zip(ys, infos)]


def kernel(x, c, ctx, c_ctx, w_ada, b_ada, norm_g, final_g, w_in_e, g_qa, w_uq, g_kv, w_ukv, w_pool, pool_scale,
           w_out_e, w_in_o, conv_w, conv_b, w_rg_a, b_rg_a, w_rg_x, b_rg_x, lru_lambda, w_out_o, w_router, w_e1,
           w_e3, w_e2):
    b, t, d = x.shape
    tc = ctx.shape[1]
    depth = w_ada.shape[0]
    n_heads = d // (2 * V_HEAD)
    pool_w = d // 2

    c_rows = jnp.concatenate([c, c_ctx[None], jnp.zeros((SUBLANES - b - 1, d), F32)], axis=0)
    mods = _mods(c_rows, w_ada, b_ada)

    def mod_rows(layer, k):
        m = mods[layer, :, k * d:(k + 1) * d]
        return m[:b, None, :], jnp.broadcast_to(m[b][None, None, :], (b, 1, d))

    cs_lat = _rope_tables(t)
    cs_ctx = jnp.concatenate([jnp.ones((tc, QK_ROPE), F32), jnp.zeros((tc, QK_ROPE), F32)], axis=-1)

    cs = ctx
    for layer in range(depth):
        last = layer == depth - 1
        (sh_m, csh_m), (sc_m, csc_m), (gt_m, cgt_m), (sh_f, csh_f), (sc_f, csc_f), (gt_f, cgt_f) = (
            mod_rows(layer, k) for k in range(N_MOD))
        g1 = norm_g[layer, 0][None]
        g2 = norm_g[layer, 1][None]
        if layer % 2 == 0:
            e = layer // 2
            wi = w_in_e[e]
            o_kpe = Q_LORA + KV_LORA
            kpe = wi[:, o_kpe:o_kpe + QK_ROPE]
            win = jnp.concatenate([wi[:, :o_kpe], wi[:, o_kpe + QK_ROPE:], kpe, _rot_columns(kpe)], axis=1).astype(BF16)
            wq3 = w_uq[e].reshape(Q_LORA, n_heads, QK_HEAD)
            wq = jnp.concatenate([wq3, _rot_columns(wq3[..., QK_NOPE:])], axis=-1).reshape(Q_LORA, n_heads * HEAD_PAD)
            wkv3 = w_ukv[e].reshape(KV_LORA, n_heads, QK_NOPE + V_HEAD)
            wk = wkv3[..., :QK_NOPE].reshape(KV_LORA, n_heads * QK_NOPE).astype(BF16)
            wv = wkv3[..., QK_NOPE:].reshape(KV_LORA, n_heads * V_HEAD).astype(BF16)
            pre = functools.partial(_even_pre, g=g1, win=win, gqa=g_qa[e][None], gkv=g_kv[e][None],
                                    wq=wq.astype(BF16), wk=wk, wv=wv, n_heads=n_heads)
            q_l, k_l, v_l, u_l = pre(x, sc_m, sh_m, cs_tab=cs_lat)
            q_c, k_c, v_c, u_c = pre(cs, csc_m, csh_m, cs_tab=cs_ctx)
            wo = w_out_e[e].astype(BF16)
            wo_a, wo_p = wo[:n_heads * V_HEAD], wo[n_heads * V_HEAD:]
            a_l = _attention(q_l, [(k_c, v_c), (k_l, v_l)], n_heads)
            pl_l = _pool(u_l, w_pool[e].astype(BF16), pool_scale[e][None])
            x, h_l, lg_l = _mix_out([a_l, pl_l], [wo_a, wo_p], x, gt_m, g2, sc_f, sh_f, w_router[layer].T)
            streams = [(h_l, lg_l)]
            if not last:
                a_c = _attention(q_c, [(k_c, v_c)], n_heads)
                pl_c = _pool(u_c, w_pool[e].astype(BF16), pool_scale[e][None])
                cs, h_c, lg_c = _mix_out([a_c, pl_c], [wo_a, wo_p], cs, cgt_m, g2, csc_f, csh_f, w_router[layer].T)
                streams.append((h_c, lg_c))
        else:
            o = layer // 2
            wi = w_in_o[o].astype(BF16)
            lw = wi.shape[1] // 2
            p_l = _norm_matmul(x, sc_m, sh_m, g1, wi)
            p_c = _norm_matmul(cs, csc_m, csh_m, g1, wi if not last else wi[:, lw:])
            wo = w_out_o[o].astype(BF16)
            y_l = _rglru(p_l, p_c[..., p_c.shape[-1] - lw:], conv_w[o], conv_b[o], w_rg_a[o].astype(BF16), b_rg_a[o],
                         w_rg_x[o].astype(BF16), b_rg_x[o], lru_lambda[o])
            x, h_l, lg_l = _mix_out([y_l], [wo], x, gt_m, g2, sc_f, sh_f, w_router[layer].T)
            streams = [(h_l, lg_l)]
            if not last:
                raise NotImplementedError("context output of an RG-LRU layer is only needed when it is not the last layer")
        outs = _moe(streams, w_e1, w_e3, w_e2, layer)
        y_l, kept_l, tab_l = outs[0]
        x = _combine(x, gt_f, kept_l, y_l, tab_l, final_g[None], final=last)
        if not last:
            y_c, kept_c, tab_c = outs[1]
            cs = _combine(cs, cgt_f, kept_c, y_c, tab_c, final_g[None], final=False)
    return x
```

```python
import functools
import math

import jax
import jax.numpy as jnp
from jax import lax
from jax.experimental import pallas as pl
from jax.experimental.pallas import tpu as pltpu

F32 = jnp.float32
BF16 = jnp.bfloat16

RMS_EPS = 1e-6
N_MOD = 6
GRID_W = 64
ROPE_BASE = 10000.0

V_HEAD = 128
QK_NOPE = 128
QK_ROPE = 64
QK_HEAD = QK_NOPE + QK_ROPE
Q_LORA = 512
KV_LORA = 512
ROPE_AXIS = QK_ROPE // 2
HEAD_PAD = 256
KEY_CHUNK = 256

POOL_WINDOWS = (2, 4, 8, 16)
POOL_PAD = 16

LRU_BLOCKS = 8
CONV_W = 4
CONV_PAD = 8
LRU_C = 8.0

N_EXPERTS = 16
EC_FACTOR = 2
MOE_ROW_TILES = 2
MOE_TOKEN_TILE = 256
MOE_WINDOW = 64

DMA_ISSUE_UNROLL = 8
SCAN_UNROLL = 8

SUBLANES = 8
LANES = 128
VMEM_LIMIT_BYTES = 56 * 1024 * 1024


def _params(n_grid):
    return pltpu.CompilerParams(dimension_semantics=("arbitrary",) * n_grid,
                                vmem_limit_bytes=VMEM_LIMIT_BYTES)


def _const_spec(shape):
    nd = len(shape)
    return pl.BlockSpec(shape, lambda *_: (0,) * nd, pipeline_mode=pl.Buffered(1))


def _rms_scale(xf):
    return xf * lax.rsqrt(jnp.mean(xf * xf, axis=-1, keepdims=True) + RMS_EPS)


def _dot(a, b):
    return jnp.dot(a, b, preferred_element_type=F32)


def _dot_nt(a, b):
    return lax.dot_general(a, b, (((1,), (1,)), ((), ())), preferred_element_type=F32)


def _row_tile(t, pref):
    return pref if t % pref == 0 else t


def _mods_kernel(c_ref, w_ref, b_ref, o_ref):
    c = c_ref[...]
    s = c * jax.nn.sigmoid(c)
    o_ref[0] = _dot(s.astype(BF16), w_ref[0].astype(BF16)) + b_ref[0]


def _mods(c_rows, w_ada, b_ada):
    depth, d, n = w_ada.shape
    tn = 1024
    return pl.pallas_call(
        _mods_kernel,
        grid=(depth, n // tn),
        in_specs=[_const_spec(c_rows.shape),
                  pl.BlockSpec((1, d, tn), lambda l, j: (l, 0, j)),
                  pl.BlockSpec((1, 1, tn), lambda l, j: (l, 0, j))],
        out_specs=pl.BlockSpec((1, c_rows.shape[0], tn), lambda l, j: (l, 0, j)),
        out_shape=jax.ShapeDtypeStruct((depth, c_rows.shape[0], n), F32),
        compiler_params=_params(2),
        name="adaln_mods",
    )(c_rows, w_ada, b_ada.reshape(depth, 1, n))


def _even_pre_kernel(x_ref, sc_ref, sh_ref, g_ref, win_ref, gqa_ref, gkv_ref, wq_ref, wk_ref, wv_ref, cs_ref,
                     q_ref, k_ref, v_ref, u_ref, *, n_heads, scale, pool_w):
    h = _rms_scale(x_ref[0]) * (g_ref[...] * (1.0 + sc_ref[0])) + sh_ref[0]
    p = _dot(h.astype(BF16), win_ref[...])
    cs = cs_ref[...]
    o_pool = Q_LORA + KV_LORA
    o_rope = o_pool + pool_w

    ql = _rms_scale(p[:, :Q_LORA]) * gqa_ref[...]
    q = _dot(ql.astype(BF16), wq_ref[...])
    for hd in range(n_heads):
        base = hd * HEAD_PAD
        hi = q[:, base + QK_NOPE:base + HEAD_PAD] * cs
        hi = hi + pltpu.roll(hi, QK_ROPE, axis=1)
        q_ref[0, :, base:base + QK_NOPE] = (q[:, base:base + QK_NOPE] * scale).astype(BF16)
        q_ref[0, :, base + QK_NOPE:base + HEAD_PAD] = (hi * scale).astype(BF16)

    kvl = (_rms_scale(p[:, Q_LORA:o_pool]) * gkv_ref[...]).astype(BF16)
    kn = _dot(kvl, wk_ref[...])
    v_ref[0] = _dot(kvl, wv_ref[...]).astype(BF16)
    kp = p[:, o_rope:o_rope + 2 * QK_ROPE] * cs
    kp = kp + pltpu.roll(kp, QK_ROPE, axis=1)
    lane = lax.broadcasted_iota(jnp.int32, kp.shape, 1)
    kp = jnp.where(lane < QK_ROPE, kp, 0.0).astype(BF16)
    for hd in range(n_heads):
        base = hd * HEAD_PAD
        k_ref[0, :, base:base + QK_NOPE] = kn[:, hd * QK_NOPE:(hd + 1) * QK_NOPE].astype(BF16)
        k_ref[0, :, base + QK_NOPE:base + HEAD_PAD] = kp
    u_ref[0] = p[:, o_pool:o_rope]


def _even_pre(x, sc, sh, g, win, gqa, gkv, wq, wk, wv, cs_tab, n_heads):
    b, t, d = x.shape
    tm = _row_tile(t, 512)
    pool_w = d // 2
    kern = functools.partial(_even_pre_kernel, n_heads=n_heads, scale=QK_HEAD ** -0.5 * math.log2(math.e),
                             pool_w=pool_w)
    tok = lambda w: pl.BlockSpec((1, tm, w), lambda bi, i: (bi, i, 0))
    per_b = pl.BlockSpec((1, 1, d), lambda bi, i: (bi, 0, 0))
    return pl.pallas_call(
        kern,
        grid=(b, t // tm),
        in_specs=[tok(d), per_b, per_b, _const_spec(g.shape), _const_spec(win.shape), _const_spec(gqa.shape),
                  _const_spec(gkv.shape), _const_spec(wq.shape), _const_spec(wk.shape), _const_spec(wv.shape),
                  pl.BlockSpec((tm, 2 * QK_ROPE), lambda bi, i: (i, 0))],
        out_specs=[tok(n_heads * HEAD_PAD), tok(n_heads * HEAD_PAD), tok(n_heads * V_HEAD), tok(pool_w)],
        out_shape=[jax.ShapeDtypeStruct((b, t, n_heads * HEAD_PAD), BF16),
                   jax.ShapeDtypeStruct((b, t, n_heads * HEAD_PAD), BF16),
                   jax.ShapeDtypeStruct((b, t, n_heads * V_HEAD), BF16),
                   jax.ShapeDtypeStruct((b, t, pool_w), F32)],
        compiler_params=_params(2),
        name="even_pre",
    )(x, sc, sh, g, win, gqa, gkv, wq, wk, wv, cs_tab)


def _attn_kernel(*refs, chunks):
    q_ref, o_ref = refs[0], refs[-1]
    q = q_ref[0]
    m = l = acc = None
    for i, s0, n in chunks:
        s = _dot_nt(q, refs[1 + 2 * i][0, s0:s0 + n, :])
        v = refs[2 + 2 * i][0, s0:s0 + n, :]
        mc = jnp.max(s, axis=-1, keepdims=True)
        if m is None:
            m_new = mc
            p = jnp.exp2(s - m_new)
            l = jnp.sum(p, axis=-1, keepdims=True)
            acc = _dot(p.astype(BF16), v)
        else:
            m_new = jnp.maximum(m, mc)
            alpha = jnp.exp2(m - m_new)
            p = jnp.exp2(s - m_new)
            l = alpha * l + jnp.sum(p, axis=-1, keepdims=True)
            acc = alpha * acc + _dot(p.astype(BF16), v)
        m = m_new
    o_ref[0] = (acc * (1.0 / l)).astype(BF16)


def _attention(q, kvs, n_heads):
    b, tq_all, _ = q.shape
    tq = _row_tile(tq_all, 4096)
    in_specs = [pl.BlockSpec((1, tq, HEAD_PAD), lambda bi, hd, i: (bi, i, hd))]
    args = [q]
    chunks = []
    for j, (k, v) in enumerate(kvs):
        s = k.shape[1]
        tk = _row_tile(s, KEY_CHUNK)
        chunks += [(j, s0, tk) for s0 in range(0, s, tk)]
        in_specs.append(pl.BlockSpec((1, s, HEAD_PAD), lambda bi, hd, i: (bi, 0, hd)))
        in_specs.append(pl.BlockSpec((1, s, V_HEAD), lambda bi, hd, i: (bi, 0, hd)))
        args += [k, v]
    return pl.pallas_call(
        functools.partial(_attn_kernel, chunks=tuple(chunks)),
        grid=(b, n_heads, tq_all // tq),
        in_specs=in_specs,
        out_specs=pl.BlockSpec((1, tq, V_HEAD), lambda bi, hd, i: (bi, i, hd)),
        out_shape=jax.ShapeDtypeStruct((b, tq_all, n_heads * V_HEAD), BF16),
        compiler_params=_params(3),
        name="mla_attention",
    )(*args)


def _pool_kernel(u_ref, w_ref, ps_ref, o_ref, e0, e1, *, t, windows):
    gi = pl.program_id(1)
    gc = u_ref.shape[-1]
    u = u_ref[0]
    zpad = jnp.zeros((POOL_PAD, gc), F32)
    for e in (e0, e1):
        e[0:POOL_PAD, :] = zpad
        e[POOL_PAD + t:2 * POOL_PAD + t, :] = zpad
    e0[POOL_PAD:POOL_PAD + t, :] = u
    pos = lax.broadcasted_iota(jnp.int32, (t, 1), 0)
    n = t + 2 * POOL_PAD - 2 * SUBLANES
    for k, w in enumerate(windows):
        @pl.when(gi == k)
        def _(w=w):
            src, dst = e0, e1
            dst[SUBLANES:SUBLANES + n, :] = src[SUBLANES - 1:SUBLANES - 1 + n, :] + src[SUBLANES:SUBLANES + n, :]
            src, dst = dst, src
            m = 2
            while m < w:
                hm = m // 2
                dst[SUBLANES:SUBLANES + n, :] = (src[SUBLANES - hm:SUBLANES - hm + n, :]
                                                 + src[SUBLANES + hm:SUBLANES + hm + n, :])
                src, dst = dst, src
                m *= 2
            tot = src[POOL_PAD:POOL_PAD + t, :]
            cnt = (jnp.minimum(pos + (w - w // 2), t) - jnp.maximum(pos - w // 2, 0)).astype(F32)
            pooled = tot / cnt - u
            o_ref[0] = (_dot(pooled.astype(BF16), w_ref[0]) * ps_ref[...]).astype(BF16)


def _pool(u, w_pool, pool_scale):
    b, t, pw = u.shape
    ng = len(POOL_WINDOWS)
    gc = pw // ng
    return pl.pallas_call(
        functools.partial(_pool_kernel, t=t, windows=POOL_WINDOWS),
        grid=(b, ng),
        in_specs=[pl.BlockSpec((1, t, gc), lambda bi, g: (bi, 0, g)),
                  pl.BlockSpec((1, gc, gc), lambda bi, g: (g, 0, 0)),
                  pl.BlockSpec((1, gc), lambda bi, g: (0, g))],
        out_specs=pl.BlockSpec((1, t, gc), lambda bi, g: (bi, 0, g)),
        out_shape=jax.ShapeDtypeStruct((b, t, pw), BF16),
        scratch_shapes=[pltpu.VMEM((t + 2 * POOL_PAD, gc), F32), pltpu.VMEM((t + 2 * POOL_PAD, gc), F32)],
        compiler_params=_params(2),
        name="pool_mixer",
    )(u, w_pool, pool_scale)


def _mix_out_kernel(*refs, n_a):
    a_refs, w_refs = refs[:n_a], refs[n_a:2 * n_a]
    x_ref, gt_ref, g2_ref, sc_ref, sh_ref, wr_ref, xo_ref, h_ref, lg_ref = refs[2 * n_a:]
    m = functools.reduce(jnp.add, [_dot(a_refs[i][0], w_refs[i][...]) for i in range(n_a)])
    xn = x_ref[0] + gt_ref[0] * m
    xo_ref[0] = xn
    hf = _rms_scale(xn) * (g2_ref[...] * (1.0 + sc_ref[0])) + sh_ref[0]
    h_ref[0] = hf
    hb = hf.astype(BF16)
    h_lo = (hf - hb.astype(F32)).astype(BF16)
    wr = wr_ref[...]
    w_hi = wr.astype(BF16)
    w_lo = (wr - w_hi.astype(F32)).astype(BF16)
    ne = wr.shape[0]
    both = _dot_nt(jnp.concatenate([w_hi, w_lo], axis=0), hb)
    lg_ref[0] = both[:ne] + (both[ne:] + _dot_nt(w_hi, h_lo))


def _mix_out(a_list, w_list, x, gt, g2, sc, sh, w_router_t):
    b, t, d = x.shape
    tm = _row_tile(t, 512)
    ne = w_router_t.shape[0]
    tok = lambda w: pl.BlockSpec((1, tm, w), lambda bi, i: (bi, i, 0))
    per_b = pl.BlockSpec((1, 1, d), lambda bi, i: (bi, 0, 0))
    in_specs = ([tok(a.shape[-1]) for a in a_list] + [_const_spec(w.shape) for w in w_list]
                + [tok(d), per_b, _const_spec(g2.shape), per_b, per_b, _const_spec(w_router_t.shape)])
    return pl.pallas_call(
        functools.partial(_mix_out_kernel, n_a=len(a_list)),
        grid=(b, t // tm),
        in_specs=in_specs,
        out_specs=[tok(d), tok(d), pl.BlockSpec((1, ne, tm), lambda bi, i: (bi, 0, i))],
        out_shape=[jax.ShapeDtypeStruct((b, t, d), F32), jax.ShapeDtypeStruct((b, t, d), F32),
                   jax.ShapeDtypeStruct((b, ne, t), F32)],
        compiler_params=_params(2),
        name="mix_out",
    )(*a_list, *w_list, x, gt, g2, sc, sh, w_router_t)


def _norm_matmul_kernel(x_ref, sc_ref, sh_ref, g_ref, w_ref, o_ref):
    h = _rms_scale(x_ref[0]) * (g_ref[...] * (1.0 + sc_ref[0])) + sh_ref[0]
    o_ref[0] = _dot(h.astype(BF16), w_ref[...])


def _norm_matmul(x, sc, sh, g, w):
    b, t, d = x.shape
    n = w.shape[1]
    tm = _row_tile(t, 512)
    tn = _row_tile(n, 2048)
    return pl.pallas_call(
        _norm_matmul_kernel,
        grid=(n // tn, b, t // tm),
        in_specs=[pl.BlockSpec((1, tm, d), lambda j, bi, i: (bi, i, 0)),
                  pl.BlockSpec((1, 1, d), lambda j, bi, i: (bi, 0, 0)),
                  pl.BlockSpec((1, 1, d), lambda j, bi, i: (bi, 0, 0)),
                  _const_spec(g.shape),
                  pl.BlockSpec((d, tn), lambda j, bi, i: (0, j))],
        out_specs=pl.BlockSpec((1, tm, tn), lambda j, bi, i: (bi, i, j)),
        out_shape=jax.ShapeDtypeStruct((b, t, n), F32),
        compiler_params=_params(3),
        name="norm_matmul",
    )(x, sc, sh, g, w)


def _rglru_kernel(gate_ref, rec_ref, recc_ref, cw_ref, cb_ref, wa_ref, ba_ref, wx_ref, bx_ref, lam_ref, o_ref,
                  ext, extc, ubuf, ubufc, abuf, bbuf, ybuf, *, t, tc, chunk):
    c = rec_ref.shape[-1]
    zpad = jnp.zeros((CONV_PAD, c), F32)
    for e, r, n in ((ext, rec_ref, t), (extc, recc_ref, tc)):
        e[0:CONV_PAD, :] = zpad
        e[CONV_PAD + n:2 * CONV_PAD + n, :] = zpad
        e[CONV_PAD:CONV_PAD + n, :] = r[0]
    cw = cw_ref[...]
    cb = cb_ref[...]
    row = lax.broadcasted_iota(jnp.int32, (SUBLANES, c), 0)

    def gates(e_ref, u_ref, r0, n, d):
        if d == 0:
            u = cb
            for k in range(CONV_W):
                off = CONV_PAD + r0 + k - CONV_W // 2
                u = u + cw[k:k + 1] * e_ref[off:off + n, :]
            u_ref[r0:r0 + n, :] = u
        else:
            u = u_ref[r0:r0 + n, :]
        ub = u.astype(BF16)
        r = jax.nn.sigmoid(_dot(ub, wa_ref[d, 0]) + ba_ref[d:d + 1])
        i = jax.nn.sigmoid(_dot(ub, wx_ref[d, 0]) + bx_ref[d:d + 1])
        log_a = (-LRU_C * r) * jax.nn.softplus(-lam_ref[d:d + 1])
        a = jnp.exp(log_a)
        abuf[0:n, :] = a
        bbuf[0:n, :] = jnp.sqrt(1.0 - a * a) * (i * u)

    def scan(n, carry, d, emit):
        nb = n // SUBLANES

        def body(j, hc):
            jj = j if d == 0 else nb - 1 - j
            off = pl.multiple_of(jj * SUBLANES, SUBLANES)
            a = abuf[pl.ds(off, SUBLANES), :]
            bb = bbuf[pl.ds(off, SUBLANES), :]
            for s in (1, 2, 4):
                if d == 0:
                    sh, msk = s, row >= s
                else:
                    sh, msk = SUBLANES - s, row < SUBLANES - s
                a_s = jnp.where(msk, pltpu.roll(a, sh, axis=0), 1.0)
                b_s = jnp.where(msk, pltpu.roll(bb, sh, axis=0), 0.0)
                bb = a * b_s + bb
                a = a * a_s
            h = a * hc + bb
            emit(off, h)
            return h[SUBLANES - 1:SUBLANES] if d == 0 else h[0:1]

        return lax.fori_loop(0, nb, body, carry, unroll=SCAN_UNROLL)

    def run(e_ref, u_ref, n_all, d, carry, emit):
        cn = min(chunk, n_all)
        starts = list(range(0, n_all, cn))
        for r0 in (starts if d == 0 else starts[::-1]):
            gates(e_ref, u_ref, r0, cn, d)
            carry = scan(cn, carry, d, functools.partial(emit, r0))
        return carry

    def emit_none(r0, off, h):
        pass

    def emit_set(r0, off, h):
        ybuf[pl.ds(r0 + off, SUBLANES), :] = h

    def emit_add(r0, off, h):
        ybuf[pl.ds(r0 + off, SUBLANES), :] += h

    h0 = jnp.zeros((1, c), F32)
    run(ext, ubuf, t, 0, run(extc, ubufc, tc, 0, h0, emit_none), emit_set)
    run(ext, ubuf, t, 1, run(extc, ubufc, tc, 1, h0, emit_none), emit_add)
    cn = min(chunk, t)
    for r0 in range(0, t, cn):
        o_ref[0, r0:r0 + cn, :] = (ybuf[r0:r0 + cn, :] * jax.nn.gelu(gate_ref[0, r0:r0 + cn, :])).astype(BF16)


def _rglru(p_lat, p_ctx, conv_w, conv_b, w_a, b_a, w_x, b_x, lam):
    b, t, c2 = p_lat.shape
    cw = c2 // 2
    tc = p_ctx.shape[1]
    nb = w_a.shape[1]
    bs = cw // nb
    chunk = 512
    vec = lambda rows: pl.BlockSpec((rows, bs), lambda bi, n: (0, n))
    wspec = pl.BlockSpec((2, 1, bs, bs), lambda bi, n: (0, n, 0, 0))
    return pl.pallas_call(
        functools.partial(_rglru_kernel, t=t, tc=tc, chunk=chunk),
        grid=(b, nb),
        in_specs=[pl.BlockSpec((1, t, bs), lambda bi, n: (bi, 0, n)),
                  pl.BlockSpec((1, t, bs), lambda bi, n: (bi, 0, nb + n)),
                  pl.BlockSpec((1, tc, bs), lambda bi, n: (bi, 0, n)),
                  vec(CONV_W), vec(1), wspec, vec(2), wspec, vec(2), vec(2)],
        out_specs=pl.BlockSpec((1, t, bs), lambda bi, n: (bi, 0, n)),
        out_shape=jax.ShapeDtypeStruct((b, t, cw), BF16),
        scratch_shapes=[pltpu.VMEM((t + 2 * CONV_PAD, bs), F32), pltpu.VMEM((tc + 2 * CONV_PAD, bs), F32),
                        pltpu.VMEM((t, bs), F32), pltpu.VMEM((tc, bs), F32),
                        pltpu.VMEM((min(chunk, max(t, tc)), bs), F32), pltpu.VMEM((min(chunk, max(t, tc)), bs), F32),
                        pltpu.VMEM((t, bs), F32)],
        compiler_params=_params(2),
        name="rglru",
    )(p_lat, p_lat, p_ctx, conv_w, conv_b.reshape(1, cw), w_a, b_a, w_x, b_x, lam)


def _lane_excl_cumsum(x, tri, ones):
    r, t = x.shape
    run = jnp.zeros((r, LANES), F32)
    outs = []
    for k in range(t // LANES):
        blk = x[:, k * LANES:(k + 1) * LANES]
        outs.append(_dot(blk, tri) + run)
        run = run + _dot(blk, ones)
    return jnp.concatenate(outs, axis=1)


def _route_kernel(lg_ref, idx_ref, gate_ref, rank_ref, kept_ref, *, t, cap):
    bi = pl.program_id(0)
    lg = lg_ref[0]
    ne = lg.shape[0]
    ex = jnp.exp(lg - jnp.max(lg, axis=0, keepdims=True))
    aff = ex / jnp.sum(ex, axis=0, keepdims=True)

    thr = jnp.zeros((ne, 1), jnp.int32)
    for bit in range(30, -1, -1):
        cand = thr | (1 << bit)
        n_ge = jnp.sum(jnp.where(aff >= lax.bitcast_convert_type(cand, F32), 1.0, 0.0), axis=1, keepdims=True)
        thr = jnp.where(n_ge >= cap, cand, thr)
    thr_f = lax.bitcast_convert_type(thr, F32)
    gt = aff > thr_f
    eq = aff == thr_f
    need = cap - jnp.sum(jnp.where(gt, 1.0, 0.0), axis=1, keepdims=True)

    ri = lax.broadcasted_iota(jnp.int32, (LANES, LANES), 0)
    ci = lax.broadcasted_iota(jnp.int32, (LANES, LANES), 1)
    tri = jnp.where(ri < ci, 1.0, 0.0).astype(BF16)
    ones = jnp.ones((LANES, LANES), BF16)
    eq_rank = _lane_excl_cumsum(jnp.where(eq, 1.0, 0.0).astype(BF16), tri, ones)
    sel = gt | (eq & (eq_rank < need))
    rank = _lane_excl_cumsum(jnp.where(sel, 1.0, 0.0).astype(BF16), tri, ones).astype(jnp.int32)
    kept = jnp.where(sel, rank, -1)
    rank_ref[0] = rank
    kept_ref[0] = kept

    tpos = lax.broadcasted_iota(jnp.int32, (1, t), 1)
    t_hi = (tpos >> 6).astype(F32)
    t_lo = (tpos & 63).astype(F32)
    g_hi = aff.astype(BF16).astype(F32)
    g_mid = (aff - g_hi).astype(BF16).astype(F32)
    g_lo = (aff - g_hi) - g_mid
    zrows = jnp.zeros((2 * SUBLANES - 5, t), F32)
    p_iota = lax.broadcasted_iota(jnp.int32, (cap, t), 0)
    for e in range(ne):
        onehot = jnp.where(kept[e:e + 1, :] == p_iota, 1.0, 0.0).astype(BF16)
        vals = jnp.concatenate([t_hi, t_lo, g_hi[e:e + 1], g_mid[e:e + 1], g_lo[e:e + 1], zrows], axis=0).astype(BF16)
        out = _dot_nt(vals, onehot)
        idx_ref[0, e:e + 1, :] = (out[0:1] * 64.0 + out[1:2]).astype(jnp.int32) + bi * t
        gate_ref[0, e:e + 1, :] = (out[2:3] + out[3:4]) + out[4:5]


def _route(lg_t):
    b, ne, t = lg_t.shape
    cap = EC_FACTOR * t // ne
    lst = lambda dt: jax.ShapeDtypeStruct((b, ne, cap), dt)
    dense = jax.ShapeDtypeStruct((b, ne, t), jnp.int32)
    per_b = lambda s: pl.BlockSpec((1,) + s, lambda bi: (bi, 0, 0))
    return pl.pallas_call(
        functools.partial(_route_kernel, t=t, cap=cap),
        grid=(b,),
        in_specs=[per_b((ne, t))],
        out_specs=[per_b((ne, cap)), per_b((ne, cap)), per_b((ne, t)), per_b((ne, t))],
        out_shape=[lst(jnp.int32), lst(F32), dense, dense],
        compiler_params=_params(1),
        name="moe_route",
    )(lg_t)


def _window_tables(rank, cap, tm):
    b, ne, t = rank.shape
    lo = jnp.swapaxes(rank[:, :, ::tm], 1, 2)
    hi = jnp.concatenate([lo[:, 1:], jnp.full((b, 1, ne), cap, jnp.int32)], axis=1)
    span = hi - lo // SUBLANES * SUBLANES
    step = MOE_WINDOW - SUBLANES
    sweeps = jnp.maximum(jnp.max((span + step - 1) // step, axis=2), 1)
    return lo.reshape(-1), sweeps.reshape(-1)


def _moe_ffn_kernel(*refs, n_rt, part_rows):
    n_src = len(part_rows)
    idx_ref, idxn_ref, g_ref = refs[:3]
    h_refs = refs[3:3 + n_src]
    w1_ref, w3_ref, w2_ref = refs[3 + n_src:6 + n_src]
    y_refs = refs[6 + n_src:6 + 2 * n_src]
    stage, xb, acc, sem = refs[6 + 2 * n_src:]
    tr = sum(part_rows)
    f = pl.program_id(2)
    tile = pl.program_id(0) * n_rt + pl.program_id(1)
    n_tiles = pl.num_programs(0) * n_rt
    starts = [sum(part_rows[:i]) for i in range(n_src)]

    def gather(ix_ref):
        for lo, n, h_ref in zip(starts, part_rows, h_refs):
            def issue(p, c, h_ref=h_ref):
                pltpu.make_async_copy(h_ref.at[pl.ds(ix_ref[0, 0, p], 1)], stage.at[pl.ds(p, 1)], sem).start()
                return c
            lax.fori_loop(lo, lo + n, issue, 0, unroll=DMA_ISSUE_UNROLL)

    @pl.when(f == 0)
    def _():
        @pl.when(tile == 0)
        def _():
            gather(idx_ref)
        pltpu.make_async_copy(h_refs[0].at[pl.ds(0, tr)], stage, sem).wait()
        xb[...] = stage[...].astype(BF16)
        acc[...] = jnp.zeros_like(acc)

        @pl.when(tile + 1 < n_tiles)
        def _():
            gather(idxn_ref)

    x = xb[...]
    a1 = _dot(x, w1_ref[0, 0].astype(BF16))
    a3 = _dot(x, w3_ref[0, 0].astype(BF16))
    hmid = (a1 * jax.nn.sigmoid(a1) * a3).astype(BF16)
    acc[...] += _dot(hmid, w2_ref[0, 0].astype(BF16))

    @pl.when(f == pl.num_programs(2) - 1)
    def _():
        for lo, n, y_ref in zip(starts, part_rows, y_refs):
            y_ref[0] = acc[lo:lo + n, :] * g_ref[0, lo:lo + n, :]


def _moe_ffn(idx, gates, hs, w1, w3, w2, layer, part_rows):
    e, n_rt, tr = idx.shape
    d = hs[0].shape[-1]
    ff = w1.shape[-1]
    tf = _row_tile(ff, 256)
    n_src = len(hs)
    tiles = lambda a: a.reshape(e * n_rt, 1, tr)
    smem = lambda imap: pl.BlockSpec((1, 1, tr), imap, memory_space=pltpu.SMEM)
    cur = lambda ei, ri, f: (ei * n_rt + ri, 0, 0)
    nxt = lambda ei, ri, f: (jnp.minimum(ei * n_rt + ri + 1, e * n_rt - 1), 0, 0)
    any_spec = pl.BlockSpec(memory_space=pl.ANY)
    return pl.pallas_call(
        functools.partial(_moe_ffn_kernel, n_rt=n_rt, part_rows=tuple(part_rows)),
        grid=(e, n_rt, ff // tf),
        in_specs=[smem(cur), smem(nxt), pl.BlockSpec((1, tr, 1), lambda ei, ri, f: (ei, ri, 0))]
                 + [any_spec] * n_src
                 + [pl.BlockSpec((1, 1, d, tf), lambda ei, ri, f: (layer, ei, 0, f)),
                    pl.BlockSpec((1, 1, d, tf), lambda ei, ri, f: (layer, ei, 0, f)),
                    pl.BlockSpec((1, 1, tf, d), lambda ei, ri, f: (layer, ei, f, 0))],
        out_specs=[pl.BlockSpec((1, n, d), lambda ei, ri, f: (ei, ri, 0)) for n in part_rows],
        out_shape=[jax.ShapeDtypeStruct((e, n * n_rt, d), F32) for n in part_rows],
        scratch_shapes=[pltpu.VMEM((tr, d), F32), pltpu.VMEM((tr, d), BF16), pltpu.VMEM((tr, d), F32),
                        pltpu.SemaphoreType.DMA],
        compiler_params=_params(3),
        name="moe_ffn",
    )(tiles(idx), tiles(idx), gates.reshape(e, n_rt * tr, 1), *hs, w1, w3, w2)


def _combine_kernel(lo_ref, sw_ref, x_ref, gt_ref, kept_ref, y_ref, fg_ref, o_ref, ybuf, acc, sem, *, cap, final):
    bi = pl.program_id(0)
    step = bi * pl.num_programs(1) + pl.program_id(1)
    n_steps = pl.num_programs(0) * pl.num_programs(1)
    ne = kept_ref.shape[2]
    w = MOE_WINDOW
    al = SUBLANES
    rows_all = y_ref.shape[1]

    def window_start(st, e, c):
        first = (st // pl.num_programs(1)) * cap + lo_ref[st * ne + e] // al * al + c * (w - al)
        return pl.multiple_of(jnp.minimum(first, rows_all - w), al)

    def fetch(st, c, slot, e):
        return pltpu.make_async_copy(y_ref.at[e, pl.ds(window_start(st, e, c), w)],
                                     ybuf.at[slot, pl.ds(e * w, w)], sem.at[slot])

    def start_all(st, c, slot):
        for e in range(ne):
            fetch(st, c, slot, e).start()

    def wait_all(st, c, slot):
        for e in range(ne):
            fetch(st, c, slot, e).wait()

    slot = lax.rem(step, 2)

    @pl.when(step == 0)
    def _():
        start_all(step, 0, slot)

    wait_all(step, 0, slot)

    @pl.when(step + 1 < n_steps)
    def _():
        start_all(step + 1, 0, 1 - slot)

    kept = kept_ref[0]
    lane = lax.broadcasted_iota(jnp.int32, (1, 2 * w), 1)
    first_half = lane < w
    row_in = jnp.where(first_half, lane, lane - w)
    acc[...] = jnp.zeros_like(acc)

    def add_windows(c, sl):
        span = jnp.where(c == sw_ref[step] - 1, rows_all, w - al)
        pieces = []
        for e2 in range(0, ne, 2):
            pos = jnp.where(first_half, kept[:, e2:e2 + 1], kept[:, e2 + 1:e2 + 2])
            off = jnp.where(first_half, window_start(step, e2, c), window_start(step, e2 + 1, c)) - bi * cap
            own_lo = jnp.where(first_half, lo_ref[step * ne + e2] // al * al,
                               lo_ref[step * ne + e2 + 1] // al * al) + c * (w - al)
            hit = (pos - off == row_in) & (pos >= own_lo) & (pos < own_lo + span)
            pieces.append(jnp.where(hit, 1.0, 0.0).astype(BF16))
        own = jnp.concatenate(pieces, axis=1)
        acc[...] += _dot(own, ybuf[sl].astype(BF16))

    add_windows(0, slot)

    def extra(c, carry):
        start_all(step, c, slot)
        wait_all(step, c, slot)
        add_windows(c, slot)
        return carry

    lax.fori_loop(1, sw_ref[step], extra, 0)
    xn = x_ref[0] + gt_ref[0] * acc[...]
    o_ref[0] = _rms_scale(xn) * fg_ref[...] if final else xn


def _combine(x, gt, kept_t, y, tables, final_g, final):
    b, t, d = x.shape
    ne = kept_t.shape[2]
    cap = EC_FACTOR * t // ne
    tm = _row_tile(t, MOE_TOKEN_TILE)
    lo, sweeps = tables
    tok = lambda wd: pl.BlockSpec((1, tm, wd), lambda bi, i, *_: (bi, i, 0))
    return pl.pallas_call(
        functools.partial(_combine_kernel, cap=cap, final=final),
        grid_spec=pltpu.PrefetchScalarGridSpec(
            num_scalar_prefetch=2,
            grid=(b, t // tm),
            in_specs=[tok(d), pl.BlockSpec((1, 1, d), lambda bi, i, *_: (bi, 0, 0)), tok(ne),
                      pl.BlockSpec(memory_space=pl.ANY), pl.BlockSpec(final_g.shape, lambda bi, i, *_: (0, 0))],
            out_specs=tok(d),
            scratch_shapes=[pltpu.VMEM((2, ne * MOE_WINDOW, d), F32), pltpu.VMEM((tm, d), F32),
                            pltpu.SemaphoreType.DMA((2,))]),
        out_shape=jax.ShapeDtypeStruct((b, t, d), F32),
        compiler_params=_params(2),
        name="moe_combine",
    )(lo, sweeps, x, gt, kept_t, y, final_g)


def _rope_tables(t):
    rows = t // GRID_W
    row = jnp.repeat(jnp.arange(rows, dtype=F32), GRID_W)
    col = jnp.tile(jnp.arange(GRID_W, dtype=F32), rows)
    inv = ROPE_BASE ** (-jnp.arange(0, ROPE_AXIS, 2, dtype=F32) / ROPE_AXIS)
    ar = row[:, None] * inv
    ac = col[:, None] * inv
    cos = jnp.concatenate([jnp.cos(ar), jnp.cos(ar), jnp.cos(ac), jnp.cos(ac)], axis=-1)
    sin = jnp.concatenate([jnp.sin(ar), jnp.sin(ar), jnp.sin(ac), jnp.sin(ac)], axis=-1)
    return jnp.concatenate([cos, sin], axis=-1)


def _rot_columns(w):
    half = ROPE_AXIS // 2
    return jnp.concatenate([-w[..., half:ROPE_AXIS], w[..., :half],
                            -w[..., ROPE_AXIS + half:], w[..., ROPE_AXIS:ROPE_AXIS + half]], axis=-1)


def _moe(streams, w1, w3, w2, layer):
    ne = w1.shape[1]
    n_rt = MOE_ROW_TILES
    idxs, gates, hs, infos = [], [], [], []
    for h, lg_t in streams:
        b, t, d = h.shape
        idx, gate, rank, kept = _route(lg_t)
        by_tile = lambda a: jnp.swapaxes(a, 0, 1).reshape(ne, n_rt, -1)
        idxs.append(by_tile(idx))
        gates.append(by_tile(gate))
        hs.append(h.reshape(b * t, d))
        infos.append((jnp.swapaxes(kept, 1, 2), _window_tables(rank, EC_FACTOR * t // ne, _row_tile(t, MOE_TOKEN_TILE))))
    ys = _moe_ffn(jnp.concatenate(idxs, axis=2), jnp.concatenate(gates, axis=2), hs, w1, w3, w2, layer,
                  [a.shape[2] for a in idxs])
    return [(y,) + info for y, info in zip(ys, infos)]


def kernel(x, c, ctx, c_ctx, w_ada, b_ada, norm_g, final_g, w_in_e, g_qa, w_uq, g_kv, w_ukv, w_pool, pool_scale,
           w_out_e, w_in_o, conv_w, conv_b, w_rg_a, b_rg_a, w_rg_x, b_rg_x, lru_lambda, w_out_o, w_router, w_e1,
           w_e3, w_e2):
    b, t, d = x.shape
    tc = ctx.shape[1]
    depth = w_ada.shape[0]
    n_heads = d // (2 * V_HEAD)
    pool_w = d // 2

    c_rows = jnp.concatenate([c, c_ctx[None], jnp.zeros((SUBLANES - b - 1, d), F32)], axis=0)
    mods = _mods(c_rows, w_ada, b_ada)

    def mod_rows(layer, k):
        m = mods[layer, :, k * d:(k + 1) * d]
        return m[:b, None, :], jnp.broadcast_to(m[b][None, None, :], (b, 1, d))

    cs_lat = _rope_tables(t)
    cs_ctx = jnp.concatenate([jnp.ones((tc, QK_ROPE), F32), jnp.zeros((tc, QK_ROPE), F32)], axis=-1)

    cs = ctx
    for layer in range(depth):
        last = layer == depth - 1
        (sh_m, csh_m), (sc_m, csc_m), (gt_m, cgt_m), (sh_f, csh_f), (sc_f, csc_f), (gt_f, cgt_f) = (
            mod_rows(layer, k) for k in range(N_MOD))
        g1 = norm_g[layer, 0][None]
        g2 = norm_g[layer, 1][None]
        if layer % 2 == 0:
            e = layer // 2
            wi = w_in_e[e]
            o_kpe = Q_LORA + KV_LORA
            kpe = wi[:, o_kpe:o_kpe + QK_ROPE]
            win = jnp.concatenate([wi[:, :o_kpe], wi[:, o_kpe + QK_ROPE:], kpe, _rot_columns(kpe)], axis=1).astype(BF16)
            wq3 = w_uq[e].reshape(Q_LORA, n_heads, QK_HEAD)
            wq = jnp.concatenate([wq3, _rot_columns(wq3[..., QK_NOPE:])], axis=-1).reshape(Q_LORA, n_heads * HEAD_PAD)
            wkv3 = w_ukv[e].reshape(KV_LORA, n_heads, QK_NOPE + V_HEAD)
            wk = wkv3[..., :QK_NOPE].reshape(KV_LORA, n_heads * QK_NOPE).astype(BF16)
            wv = wkv3[..., QK_NOPE:].reshape(KV_LORA, n_heads * V_HEAD).astype(BF16)
            pre = functools.partial(_even_pre, g=g1, win=win, gqa=g_qa[e][None], gkv=g_kv[e][None],
                                    wq=wq.astype(BF16), wk=wk, wv=wv, n_heads=n_heads)
            q_l, k_l, v_l, u_l = pre(x, sc_m, sh_m, cs_tab=cs_lat)
            q_c, k_c, v_c, u_c = pre(cs, csc_m, csh_m, cs_tab=cs_ctx)
            wo = w_out_e[e].astype(BF16)
            wo_a, wo_p = wo[:n_heads * V_HEAD], wo[n_heads * V_HEAD:]
            a_l = _attention(q_l, [(k_c, v_c), (k_l, v_l)], n_heads)
            pl_l = _pool(u_l, w_pool[e].astype(BF16), pool_scale[e][None])
            x, h_l, lg_l = _mix_out([a_l, pl_l], [wo_a, wo_p], x, gt_m, g2, sc_f, sh_f, w_router[layer].T)
            streams = [(h_l, lg_l)]
            if not last:
                a_c = _attention(q_c, [(k_c, v_c)], n_heads)
                pl_c = _pool(u_c, w_pool[e].astype(BF16), pool_scale[e][None])
                cs, h_c, lg_c = _mix_out([a_c, pl_c], [wo_a, wo_p], cs, cgt_m, g2, csc_f, csh_f, w_router[layer].T)
                streams.append((h_c, lg_c))
        else:
            o = layer // 2
            wi = w_in_o[o].astype(BF16)
            lw = wi.shape[1] // 2
            p_l = _norm_matmul(x, sc_m, sh_m, g1, wi)
            p_c = _norm_matmul(cs, csc_m, csh_m, g1, wi if not last else wi[:, lw:])
            wo = w_out_o[o].astype(BF16)
            y_l = _rglru(p_l, p_c[..., p_c.shape[-1] - lw:], conv_w[o], conv_b[o], w_rg_a[o].astype(BF16), b_rg_a[o],
                         w_rg_x[o].astype(BF16), b_rg_x[o], lru_lambda[o])
            x, h_l, lg_l = _mix_out([y_l], [wo], x, gt_m, g2, sc_f, sh_f, w_router[layer].T)
            streams = [(h_l, lg_l)]
            if not last:
                raise NotImplementedError("context output of an RG-LRU layer is only needed when it is not the last layer")
        outs = _moe(streams, w_e1, w_e3, w_e2, layer)
        y_l, kept_l, tab_l = outs[0]
        x = _combine(x, gt_f, kept_l, y_l, tab_l, final_g[None], final=last)
        if not last:
            y_c, kept_c, tab_c = outs[1]
            cs = _combine(cs, cgt_f, kept_c, y_c, tab_c, final_g[None], final=False)
    return x
```
